```python
import math
import jax, jax.numpy as jnp
from jax import lax
import numpy as np


D_MODEL = 2048
BATCH = 1
SEQ = 8192
DEPTH = 1
DEC_BATCH = 32
DEC_SEQ = 8
PAST_LEN = 16384
PAGE_SIZE = 128

GLA_HEADS = 4
GLA_DK = 128
GLA_DV = 256
GLA_GATE_RANK = 16
GLA_GATE_TAU = 16.0
GLA_CHUNK = 64
FOX_HEADS = 8
FOX_HD = 128
Q_BLOCK = 128
FOX_FORGET_BIAS = 8.0
PEER_HEADS = 8
PEER_NKEYS = 128
PEER_EXPERTS = PEER_NKEYS * PEER_NKEYS
PEER_DQ = 256
PEER_TOPK = 16
PEER_BLOCK = 64
EPS = 1e-6

GLA_QK_W = GLA_HEADS * GLA_DK
GLA_V_W = GLA_HEADS * GLA_DV
FOX_W = FOX_HEADS * FOX_HD
IN_SPLITS = (GLA_QK_W, GLA_QK_W, GLA_V_W, GLA_V_W, GLA_GATE_RANK, FOX_W, FOX_W, FOX_W, FOX_HEADS, D_MODEL, D_MODEL)
IN_W = 2 * GLA_QK_W + 2 * GLA_V_W + GLA_GATE_RANK + 3 * FOX_W + FOX_HEADS + 2 * D_MODEL

kernel_name = "gla_fox_peer_adaln_hybrid_step"


def rmsnorm(x, g):
    xf = x.astype(jnp.float32)
    y = xf * lax.rsqrt(jnp.mean(xf * xf, axis=-1, keepdims=True) + EPS)
    return (y * g.astype(jnp.float32)).astype(x.dtype)


def split_cols(z, widths):
    out, off = [], 0
    for w in widths:
        out.append(z[..., off:off + w])
        off += w
    return out


def gla_chunked(q, k, v, loga, s0):
    B, T, H, DK = q.shape
    DV = v.shape[-1]
    C = math.gcd(T, GLA_CHUNK)
    N = T // C
    f32 = jnp.float32

    def chunks(a):
        return a.astype(f32).reshape(B, N, C, *a.shape[2:]).swapaxes(0, 1)

    causal = jnp.tril(jnp.ones((C, C), dtype=bool))[None, :, :, None, None]

    def step(S, inp):
        qc, kc, vc, ac = inp
        b = jnp.cumsum(ac, axis=1)
        dec = jnp.where(causal, b[:, :, None] - b[:, None, :], -jnp.inf)
        A = jnp.einsum('bthk,bshk,btshk->bhts', qc, kc, jnp.exp(dec))
        o = (jnp.einsum('bhts,bshv->bthv', A, vc)
             + jnp.einsum('bthk,bhkv->bthv', qc * jnp.exp(b), S))
        bl = b[:, -1]
        S = (jnp.exp(bl)[..., None] * S
             + jnp.einsum('bshk,bshv->bhkv', kc * jnp.exp(bl[:, None] - b), vc))
        return S, o

    S, o = lax.scan(step, s0.astype(f32), (chunks(q), chunks(k), chunks(v), chunks(loga)))
    return o.swapaxes(0, 1).reshape(B, T, H, DV), S


def fox_prompt_attend(q, k, v, logf):
    B, S, H, Dh = q.shape
    cum = jnp.cumsum(logf.astype(jnp.float32), axis=1).transpose(0, 2, 1)
    kpos = jnp.arange(S)
    scale = Dh ** -0.5

    def block(i):
        q0 = i * Q_BLOCK
        qb = lax.dynamic_slice_in_dim(q, q0, Q_BLOCK, axis=1)
        cb = lax.dynamic_slice_in_dim(cum, q0, Q_BLOCK, axis=2)
        s = jnp.einsum('bqhd,bkhd->bhqk', qb, k).astype(jnp.float32) * scale
        s = s + cb[..., None] - cum[:, :, None, :]
        qpos = q0 + jnp.arange(Q_BLOCK)
        s = jnp.where(kpos[None, :] <= qpos[:, None], s, -jnp.inf)
        p = jax.nn.softmax(s, axis=-1).astype(v.dtype)
        return jnp.einsum('bhqk,bkhd->bqhd', p, v)

    o = lax.map(block, jnp.arange(S // Q_BLOCK))
    return o.transpose(1, 0, 2, 3, 4).reshape(B, S, H, Dh)


def make_fox_sample_attend(cache_k, cache_v, cache_logf, page_table, layer):
    def attend(q, k, v, logf):
        B, T, H, Dh = q.shape
        scale = Dh ** -0.5
        causal = jnp.tril(jnp.ones((T, T), dtype=bool))

        def one(args):
            qs, ks, vs, lfs, pt = args
            kp = cache_k[layer, pt].reshape(-1, H, Dh)
            vp = cache_v[layer, pt].reshape(-1, H, Dh)
            lfp = cache_logf[layer, pt].reshape(-1, H).astype(jnp.float32)
            P = kp.shape[0]
            tail = lax.cumsum(lfp, axis=0, reverse=True) - lfp
            cn = jnp.cumsum(lfs.astype(jnp.float32), axis=0)
            sp = (jnp.einsum('thd,shd->hts', qs, kp).astype(jnp.float32) * scale
                  + cn.T[:, :, None] + tail.T[:, None, :])
            sn = (jnp.einsum('thd,shd->hts', qs, ks).astype(jnp.float32) * scale
                  + cn.T[:, :, None] - cn.T[:, None, :])
            sn = jnp.where(causal, sn, -jnp.inf)
            p = jax.nn.softmax(jnp.concatenate([sp, sn], axis=-1), axis=-1)
            return (jnp.einsum('hts,shd->thd', p[..., :P].astype(vp.dtype), vp)
                    + jnp.einsum('hts,shd->thd', p[..., P:].astype(vs.dtype), vs))

        return lax.map(one, (q, k, v, logf, page_table))
    return attend


def peer_ffn(h, wq, subkeys, u_tab, v_tab):
    shp = h.shape
    t = h.reshape(-1, D_MODEL)
    n = t.shape[0]
    nb = -(-n // PEER_BLOCK)
    t = jnp.pad(t, ((0, nb * PEER_BLOCK - n), (0, 0))).reshape(nb, PEER_BLOCK, D_MODEL)

    def block(tb):
        q = (tb @ wq).reshape(PEER_BLOCK, PEER_HEADS, 2, PEER_DQ // 2)
        s = jnp.einsum('thpd,hpkd->thpk', q, subkeys).astype(jnp.float32)
        sv, si = lax.top_k(s, PEER_TOPK)
        cand = (sv[:, :, 0, :, None] + sv[:, :, 1, None, :]).reshape(PEER_BLOCK, PEER_HEADS, PEER_TOPK * PEER_TOPK)
        cs, ci = lax.top_k(cand, PEER_TOPK)
        i1 = jnp.take_along_axis(si[:, :, 0], ci // PEER_TOPK, axis=-1)
        i2 = jnp.take_along_axis(si[:, :, 1], ci % PEER_TOPK, axis=-1)
        e = i1 * PEER_NKEYS + i2
        g = jax.nn.softmax(cs, axis=-1)
        a = jnp.einsum('td,thkd->thk', tb, u_tab[e]).astype(jnp.float32)
        w = (g * jax.nn.gelu(a, approximate=False)).astype(tb.dtype)
        return jnp.einsum('thk,thkd->td', w, v_tab[e])

    out = lax.map(block, t).reshape(nb * PEER_BLOCK, D_MODEL)[:n]
    return out.reshape(shp)


def trunk_layer(x, c, gla_s0, fox_attend, norm1_g, norm2_g, w_ada, b_ada, w_in, gla_f2, gla_fb, gla_on_g,
                fox_qn_g, fox_kn_g, fox_fb, w_pa, w_pb, w_out, peer_wq, peer_subkeys, peer_u, peer_v):
    B, T, _ = x.shape
    mod = (c @ w_ada + b_ada)[:, None, :]
    sh1, sc1, g1, sh2, sc2, g2 = jnp.split(mod, 6, axis=-1)
    h = rmsnorm(x, norm1_g) * (1.0 + sc1) + sh1
    gq, gk, gv, gr, gf, fq, fk, fv, ff, ma, mb = split_cols(h @ w_in, IN_SPLITS)
    gq = gq.reshape(B, T, GLA_HEADS, GLA_DK) * (GLA_DK ** -0.5)
    gk = gk.reshape(B, T, GLA_HEADS, GLA_DK)
    gv = gv.reshape(B, T, GLA_HEADS, GLA_DV)
    loga = jax.nn.log_sigmoid((gf @ gla_f2 + gla_fb).astype(jnp.float32)) / GLA_GATE_TAU
    loga = loga.reshape(B, T, GLA_HEADS, GLA_DK)
    o_gla, s_gla = gla_chunked(gq, gk, gv, loga, gla_s0)
    o_gla = rmsnorm(o_gla.astype(x.dtype), gla_on_g) * jax.nn.silu(gr.reshape(B, T, GLA_HEADS, GLA_DV))
    o_gla = o_gla.reshape(B, T, GLA_V_W)
    fq = rmsnorm(fq.reshape(B, T, FOX_HEADS, FOX_HD), fox_qn_g)
    fk = rmsnorm(fk.reshape(B, T, FOX_HEADS, FOX_HD), fox_kn_g)
    fv = fv.reshape(B, T, FOX_HEADS, FOX_HD)
    logf = jax.nn.log_sigmoid((ff + fox_fb).astype(jnp.float32))
    o_fox = fox_attend(fq, fk, fv, logf).reshape(B, T, FOX_W)
    merged = jax.nn.sigmoid(ma) * (o_gla @ w_pa) + jax.nn.sigmoid(mb) * (o_fox @ w_pb)
    x = x + g1 * (merged @ w_out)
    h2 = rmsnorm(x, norm2_g) * (1.0 + sc2) + sh2
    x = x + g2 * peer_ffn(h2, peer_wq, peer_subkeys, peer_u, peer_v)
    return x, s_gla, fk, fv, logf


def setup_inputs(seed: int = 0) -> dict:
    key = jax.random.key(seed)
    ks = jax.random.split(key, 32)
    f32 = jnp.float32
    n_pages = PAST_LEN // PAGE_SIZE
    n_used = DEC_BATCH * n_pages
    n_pool = n_used + max(1, n_used // 4)
    nrm = lambda k, shp, s: jax.random.normal(k, shp, f32) * s
    page_table = jax.random.permutation(ks[9], n_pool)[:n_used].reshape(DEC_BATCH, n_pages).astype(jnp.int32)
    return {
        "x_prompt": nrm(ks[0], (BATCH, SEQ, D_MODEL), 1.0),
        "x_sample": nrm(ks[1], (DEC_BATCH, DEC_SEQ, D_MODEL), 1.0),
        "c_prompt": nrm(ks[2], (BATCH, D_MODEL), 1.0),
        "c_sample": nrm(ks[3], (DEC_BATCH, D_MODEL), 1.0),
        "state_gla": nrm(ks[4], (DEPTH, DEC_BATCH, GLA_HEADS, GLA_DK, GLA_DV), 0.5),
        "cache_k": nrm(ks[5], (DEPTH, n_pool, PAGE_SIZE, FOX_HEADS, FOX_HD), 1.0),
        "cache_v": nrm(ks[6], (DEPTH, n_pool, PAGE_SIZE, FOX_HEADS, FOX_HD), 1.0),
        "cache_logf": jax.nn.log_sigmoid(FOX_FORGET_BIAS + 0.5 * jax.random.normal(ks[7], (DEPTH, n_pool, PAGE_SIZE, FOX_HEADS), f32)),
        "page_table": page_table,
        "norm1_g": 1.0 + nrm(ks[10], (DEPTH, D_MODEL), 0.02),
        "norm2_g": 1.0 + nrm(ks[11], (DEPTH, D_MODEL), 0.02),
        "w_ada": nrm(ks[12], (DEPTH, D_MODEL, 6 * D_MODEL), 0.3 * D_MODEL ** -0.5),
        "b_ada": nrm(ks[13], (DEPTH, 6 * D_MODEL), 0.02),
        "w_in": nrm(ks[14], (DEPTH, D_MODEL, IN_W), D_MODEL ** -0.5),
        "gla_f2": nrm(ks[15], (DEPTH, GLA_GATE_RANK, GLA_QK_W), GLA_GATE_RANK ** -0.5),
        "gla_fb": nrm(ks[16], (DEPTH, GLA_QK_W), 0.1),
        "gla_on_g": 1.0 + nrm(ks[17], (DEPTH, GLA_DV), 0.02),
        "fox_qn_g": 1.0 + nrm(ks[18], (DEPTH, FOX_HD), 0.02),
        "fox_kn_g": 1.0 + nrm(ks[19], (DEPTH, FOX_HD), 0.02),
        "fox_fb": FOX_FORGET_BIAS + nrm(ks[20], (DEPTH, FOX_HEADS), 0.1),
        "w_pa": nrm(ks[21], (DEPTH, GLA_V_W, D_MODEL), GLA_V_W ** -0.5),
        "w_pb": nrm(ks[22], (DEPTH, FOX_W, D_MODEL), FOX_W ** -0.5),
        "w_out": nrm(ks[23], (DEPTH, D_MODEL, D_MODEL), D_MODEL ** -0.5),
        "peer_wq": nrm(ks[24], (DEPTH, D_MODEL, PEER_HEADS * PEER_DQ), D_MODEL ** -0.5),
        "peer_subkeys": nrm(ks[25], (DEPTH, PEER_HEADS, 2, PEER_NKEYS, PEER_DQ // 2), (PEER_DQ // 2) ** -0.5),
        "peer_u": nrm(ks[26], (DEPTH, PEER_EXPERTS, D_MODEL), D_MODEL ** -0.5),
        "peer_v": nrm(ks[27], (DEPTH, PEER_EXPERTS, D_MODEL), (PEER_HEADS * PEER_TOPK) ** -0.5),
    }


def reference(x_prompt, x_sample, c_prompt, c_sample, state_gla, cache_k, cache_v, cache_logf, page_table,
              norm1_g, norm2_g, w_ada, b_ada, w_in, gla_f2, gla_fb, gla_on_g, fox_qn_g, fox_kn_g, fox_fb,
              w_pa, w_pb, w_out, peer_wq, peer_subkeys, peer_u, peer_v):
    xp, xs = x_prompt, x_sample
    gla_p, gla_s, k_p, v_p, lf_p, k_s, v_s, lf_s = [], [], [], [], [], [], [], []
    for l in range(DEPTH):
        w = (norm1_g[l], norm2_g[l], w_ada[l], b_ada[l], w_in[l], gla_f2[l], gla_fb[l], gla_on_g[l],
             fox_qn_g[l], fox_kn_g[l], fox_fb[l], w_pa[l], w_pb[l], w_out[l],
             peer_wq[l], peer_subkeys[l], peer_u[l], peer_v[l])
        s0 = jnp.zeros((xp.shape[0], GLA_HEADS, GLA_DK, GLA_DV), jnp.float32)
        xp, sp, kp, vp, lp = trunk_layer(xp, c_prompt, s0, fox_prompt_attend, *w)
        attend_s = make_fox_sample_attend(cache_k, cache_v, cache_logf, page_table, l)
        xs, ss, kss, vss, lss = trunk_layer(xs, c_sample, state_gla[l], attend_s, *w)
        gla_p.append(sp); k_p.append(kp); v_p.append(vp); lf_p.append(lp)
        gla_s.append(ss); k_s.append(kss); v_s.append(vss); lf_s.append(lss)
    return (xp, xs, jnp.stack(gla_p), jnp.stack(gla_s), jnp.stack(k_p), jnp.stack(v_p), jnp.stack(lf_p),
            jnp.stack(k_s), jnp.stack(v_s), jnp.stack(lf_s))
```

```python
import functools

import jax
import jax.numpy as jnp
from jax import lax
from jax.experimental import pallas as pl
from jax.experimental.pallas import tpu as pltpu

F32 = jnp.float32
BF16 = jnp.bfloat16

D_MODEL = 2048
GLA_HEADS = 4
GLA_DK = 128
GLA_DV = 256
GLA_GATE_RANK = 16
GLA_GATE_TAU = 16.0
GLA_CHUNK = 64
GLA_ROWBLOCK = 16
FOX_HEADS = 8
FOX_HD = 128
FOX_W = FOX_HEADS * FOX_HD
PEER_HEADS = 8
PEER_NKEYS = 128
PEER_TOPK = 16
PAGE_SIZE = 128
EPS = 1e-6

GLA_QK_W = GLA_HEADS * GLA_DK
GLA_V_W = GLA_HEADS * GLA_DV
_OFF_GF = 2 * GLA_QK_W + 2 * GLA_V_W
_OFF_FQ = _OFF_GF + GLA_GATE_RANK
_OFF_FK = _OFF_FQ + FOX_W
_OFF_FV = _OFF_FK + FOX_W
_OFF_FF = _OFF_FV + FOX_W
_OFF_MA = _OFF_FF + FOX_HEADS
_OFF_MB = _OFF_MA + D_MODEL
_OFF_END = _OFF_MB + D_MODEL

LANES = 128
ROW_BLOCK = 256
VMEM_LIMIT = 56 * 1024 * 1024

NT = (((1,), (1,)), ((), ()))
NEG_INF = float("-inf")


def _cparams(sem):
    return pltpu.CompilerParams(dimension_semantics=sem, vmem_limit_bytes=VMEM_LIMIT)


def _split3(x):
    hi = x.astype(BF16)
    r = x - hi.astype(F32)
    mid = r.astype(BF16)
    lo = (r - mid.astype(F32)).astype(BF16)
    return hi, mid, lo


def _dot01(m01, x):
    acc = None
    for p in _split3(x):
        t = jnp.dot(m01, p, preferred_element_type=F32)
        acc = t if acc is None else acc + t
    return acc


def _dotx01(x, m01):
    acc = None
    for p in _split3(x):
        t = jnp.dot(p, m01, preferred_element_type=F32)
        acc = t if acc is None else acc + t
    return acc


def _log_sigmoid(z):
    return jnp.minimum(z, 0.0) - jnp.log1p(jnp.exp(-jnp.abs(z)))


def _rms(x, g):
    return x * lax.rsqrt(jnp.mean(x * x, axis=-1, keepdims=True) + EPS) * g


def _ada_kernel(c_ref, w_ref, b_ref, o_ref):
    o_ref[...] = jnp.dot(c_ref[...].astype(BF16), w_ref[...].astype(BF16),
                         preferred_element_type=F32) + b_ref[...]


def _ada(c_all, w_ada, b_ada):
    m, k = c_all.shape
    n = w_ada.shape[1]
    tn = 1024
    return pl.pallas_call(
        _ada_kernel,
        out_shape=jax.ShapeDtypeStruct((m, n), F32),
        grid=(n // tn,),
        in_specs=[pl.BlockSpec((m, k), lambda j: (0, 0)),
                  pl.BlockSpec((k, tn), lambda j: (0, j)),
                  pl.BlockSpec((1, tn), lambda j: (0, j))],
        out_specs=pl.BlockSpec((m, tn), lambda j: (0, j)),
        compiler_params=_cparams(("parallel",)),
        name="ada_mod",
    )(c_all, w_ada, b_ada.reshape(1, n))


def _normmod_kernel(xp_ref, xs_ref, g_ref, scp_ref, shp_ref, scs_ref, shs_ref, o_ref, *, n_prompt_blocks):
    i = pl.program_id(0)

    @pl.when(i < n_prompt_blocks)
    def _():
        h = _rms(xp_ref[...], g_ref[...]) * (1.0 + scp_ref[...]) + shp_ref[...]
        o_ref[...] = h.astype(o_ref.dtype)

    @pl.when(i >= n_prompt_blocks)
    def _():
        h = _rms(xs_ref[...], g_ref[...]) * (1.0 + scs_ref[...]) + shs_ref[...]
        o_ref[...] = h.astype(o_ref.dtype)


def _normmod(xp, xs, g, scp, shp, scs, shs):
    tp, d = xp.shape
    ts = xs.shape[0]
    tb = ROW_BLOCK
    npb, nsb = tp // tb, ts // tb
    pidx = lambda i: (jnp.minimum(i, npb - 1), 0)
    sidx = lambda i: (jnp.maximum(i - npb, 0), 0)
    row = pl.BlockSpec((1, d), lambda i: (0, 0))
    return pl.pallas_call(
        functools.partial(_normmod_kernel, n_prompt_blocks=npb),
        out_shape=jax.ShapeDtypeStruct((tp + ts, d), BF16),
        grid=(npb + nsb,),
        in_specs=[pl.BlockSpec((tb, d), pidx), pl.BlockSpec((tb, d), sidx), row, row, row,
                  pl.BlockSpec((tb, d), sidx), pl.BlockSpec((tb, d), sidx)],
        out_specs=pl.BlockSpec((tb, d), lambda i: (i, 0)),
        compiler_params=_cparams(("parallel",)),
        name="norm_mod",
    )(xp, xs, g, scp, shp, scs, shs)


def _mm_kernel(a_ref, w_ref, o_ref):
    o_ref[...] = jnp.dot(a_ref[...], w_ref[...], preferred_element_type=F32).astype(o_ref.dtype)


def _pick(n, cands):
    for c in cands:
        if n % c == 0:
            return c
    return n


def _matmul(a, w, out_dtype, name):
    m, k = a.shape
    n = w.shape[1]
    tm = _pick(m, (1024, 768, 512, 256))
    tn = _pick(n, (1024, 512, 256, 128))
    return pl.pallas_call(
        _mm_kernel,
        out_shape=jax.ShapeDtypeStruct((m, n), out_dtype),
        grid=(m // tm, n // tn),
        in_specs=[pl.BlockSpec((tm, k), lambda i, j: (i, 0)),
                  pl.BlockSpec((k, tn), lambda i, j: (0, j))],
        out_specs=pl.BlockSpec((tm, tn), lambda i, j: (i, j)),
        compiler_params=_cparams(("parallel", "parallel")),
        name=name,
    )(a, w)


def _fox_prep_kernel(fq_ref, fkv_ref, zs_ref, qg_ref, kg_ref, fb_ref,
                     qn_ref, kb_ref, vb_ref, cum_ref, kp_ref, vp_ref, lfp_ref, ks_ref, vs_ref, lfs_ref,
                     carry_ref, *, n_prompt_blocks, seg_len):
    i = pl.program_id(0)
    tb = fq_ref.shape[0]
    is_prompt = i < n_prompt_blocks

    @pl.when(i == 0)
    def _():
        carry_ref[...] = jnp.zeros_like(carry_ref)

    scale = FOX_HD ** -0.5
    fq = fq_ref[...].astype(F32)
    fkv = fkv_ref[...]
    kn = []
    for h in range(FOX_HEADS):
        sl = slice(h * FOX_HD, (h + 1) * FOX_HD)
        qn_ref[:, sl] = (_rms(fq[:, sl], qg_ref[...]) * scale).astype(qn_ref.dtype)
        kn.append(_rms(fkv[:, sl], kg_ref[...]))
    kn = jnp.concatenate(kn, axis=1)
    v = fkv[:, FOX_W:]
    kb_ref[...] = kn.astype(kb_ref.dtype)
    vb_ref[...] = v.astype(vb_ref.dtype)

    lf = _log_sigmoid(zs_ref[...] + fb_ref[...])
    row = lax.broadcasted_iota(jnp.int32, (tb, tb), 0)
    col = lax.broadcasted_iota(jnp.int32, (tb, tb), 1)

    @pl.when(is_prompt)
    def _():
        tri = jnp.where(col <= row, 1.0, 0.0).astype(BF16)
        cum = _dot01(tri, lf) + carry_ref[...]
        carry_ref[...] = cum[tb - 1:tb, :]
        cum_ref[...] = cum
        kp_ref[...] = kn
        vp_ref[...] = v
        lfp_ref[...] = lf

    @pl.when(jnp.logical_not(is_prompt))
    def _():
        tri = jnp.where(col <= row, jnp.where((row // seg_len) == (col // seg_len), 1.0, 0.0), 0.0).astype(BF16)
        cum_ref[...] = _dot01(tri, lf)
        ks_ref[...] = kn
        vs_ref[...] = v
        lfs_ref[...] = lf


def _fox_prep(z1, z2, zs, qg, kg, fb_pad, tp, seg_len):
    t_all = z1.shape[0]
    ts = t_all - tp
    tb = ROW_BLOCK
    npb, nb = tp // tb, t_all // tb
    pidx = lambda i: (jnp.minimum(i, npb - 1), 0)
    sidx = lambda i: (jnp.maximum(i - npb, 0), 0)
    allidx = lambda i: (i, 0)
    row = pl.BlockSpec((1, LANES), lambda i: (0, 0))
    fq_block = (_OFF_GF) // FOX_W
    return pl.pallas_call(
        functools.partial(_fox_prep_kernel, n_prompt_blocks=npb, seg_len=seg_len),
        out_shape=(jax.ShapeDtypeStruct((t_all, FOX_W), BF16),
                   jax.ShapeDtypeStruct((t_all, FOX_W), BF16),
                   jax.ShapeDtypeStruct((t_all, FOX_W), BF16),
                   jax.ShapeDtypeStruct((t_all, LANES), F32),
                   jax.ShapeDtypeStruct((tp, FOX_W), F32),
                   jax.ShapeDtypeStruct((tp, FOX_W), F32),
                   jax.ShapeDtypeStruct((tp, LANES), F32),
                   jax.ShapeDtypeStruct((ts, FOX_W), F32),
                   jax.ShapeDtypeStruct((ts, FOX_W), F32),
                   jax.ShapeDtypeStruct((ts, LANES), F32)),
        grid=(nb,),
        in_specs=[pl.BlockSpec((tb, FOX_W), lambda i: (i, fq_block)),
                  pl.BlockSpec((tb, 2 * FOX_W), allidx),
                  pl.BlockSpec((tb, LANES), allidx), row, row, row],
        out_specs=(pl.BlockSpec((tb, FOX_W), allidx), pl.BlockSpec((tb, FOX_W), allidx),
                   pl.BlockSpec((tb, FOX_W), allidx), pl.BlockSpec((tb, LANES), allidx),
                   pl.BlockSpec((tb, FOX_W), pidx), pl.BlockSpec((tb, FOX_W), pidx),
                   pl.BlockSpec((tb, LANES), pidx),
                   pl.BlockSpec((tb, FOX_W), sidx), pl.BlockSpec((tb, FOX_W), sidx),
                   pl.BlockSpec((tb, LANES), sidx)),
        scratch_shapes=[pltpu.VMEM((1, LANES), F32)],
        compiler_params=_cparams(("arbitrary",)),
        name="fox_prep",
    )(z1, z2, zs, qg, kg, fb_pad)


def _fox_prompt_kernel(qi_ref, ki_ref, q_ref, k_ref, v_ref, nc_ref, o_ref, m_ref, l_ref, acc_ref):
    step = pl.program_id(1)
    qi = qi_ref[step]
    ki = ki_ref[step]
    tq, tk = q_ref.shape[0], k_ref.shape[0]

    @pl.when(ki == 0)
    def _():
        m_ref[...] = jnp.full_like(m_ref, NEG_INF)
        l_ref[...] = jnp.zeros_like(l_ref)
        acc_ref[...] = jnp.zeros_like(acc_ref)

    s = lax.dot_general(q_ref[...], k_ref[...], NT, preferred_element_type=F32) + nc_ref[...]
    row = qi * tq + lax.broadcasted_iota(jnp.int32, (tq, tk), 0)
    col = ki * tk + lax.broadcasted_iota(jnp.int32, (tq, tk), 1)
    s = jnp.where(col <= row, s, NEG_INF)
    m_prev = m_ref[...]
    m_new = jnp.maximum(m_prev, jnp.max(s, axis=1, keepdims=True))
    alpha = jnp.exp(m_prev - m_new)
    p = jnp.exp(s - m_new)
    l_ref[...] = alpha * l_ref[...] + jnp.sum(p, axis=1, keepdims=True)
    acc_ref[...] = alpha * acc_ref[...] + jnp.dot(p.astype(BF16), v_ref[...], preferred_element_type=F32)
    m_ref[...] = m_new

    @pl.when(ki == qi)
    def _():
        o_ref[...] = (acc_ref[...] / l_ref[...]).astype(o_ref.dtype)


def _fox_prompt(qn, kb, vb, neg_cum_t, tp):
    tq = tk = _pick(tp, (512, 256, 128))
    nq = tp // tq
    qi_tab = jnp.asarray([q for q in range(nq) for _ in range(q + 1)], jnp.int32)
    ki_tab = jnp.asarray([k for q in range(nq) for k in range(q + 1)], jnp.int32)
    grid_spec = pltpu.PrefetchScalarGridSpec(
        num_scalar_prefetch=2,
        grid=(FOX_HEADS, int(qi_tab.shape[0])),
        in_specs=[pl.BlockSpec((tq, FOX_HD), lambda h, s, qi, ki: (qi[s], h)),
                  pl.BlockSpec((tk, FOX_HD), lambda h, s, qi, ki: (ki[s], h)),
                  pl.BlockSpec((tk, FOX_HD), lambda h, s, qi, ki: (ki[s], h)),
                  pl.BlockSpec((None, 1, tk), lambda h, s, qi, ki: (h, 0, ki[s]))],
        out_specs=pl.BlockSpec((tq, FOX_HD), lambda h, s, qi, ki: (qi[s], h)),
        scratch_shapes=[pltpu.VMEM((tq, 1), F32), pltpu.VMEM((tq, 1), F32), pltpu.VMEM((tq, FOX_HD), F32)],
    )
    return pl.pallas_call(
        _fox_prompt_kernel,
        out_shape=jax.ShapeDtypeStruct((tp, FOX_W), BF16),
        grid_spec=grid_spec,
        compiler_params=_cparams(("parallel", "arbitrary")),
        name="fox_prompt",
    )(qi_tab, ki_tab, qn, kb, vb, neg_cum_t)


def _expand_heads(x_t, n_q):
    hi, mid, lo = _split3(x_t)
    n = x_t.shape[0]
    stacked = jnp.concatenate([hi.astype(F32), mid.astype(F32), lo.astype(F32),
                               jnp.zeros((LANES - 3 * n, LANES), F32)], axis=0)
    k = lax.broadcasted_iota(jnp.int32, (LANES, LANES), 0)
    c = lax.broadcasted_iota(jnp.int32, (LANES, LANES), 1)
    e3 = jnp.where(k < 3 * n, jnp.where(c < n * n_q, jnp.where((k % n) == (c // n_q), 1.0, 0.0), 0.0), 0.0)
    return jnp.dot(stacked.T.astype(BF16), e3.astype(BF16), preferred_element_type=F32)


def _fox_sample_kernel(pt_ref, qbd_ref, kn_ref, vn_ref, lfn_ref, *rest, pages_per_step, n_groups):
    pg = pages_per_step
    k_refs = rest[:pg]
    lf_refs = rest[pg:2 * pg]
    v_refs = rest[2 * pg:3 * pg]
    o_ref = rest[3 * pg]
    s_all, s_new, m_ref, l_ref, acc_ref, carry_ref = rest[3 * pg + 1:]
    j = pl.program_id(1)
    n_new = kn_ref.shape[0]
    ki = lax.broadcasted_iota(jnp.int32, (LANES, LANES), 0)
    ci = lax.broadcasted_iota(jnp.int32, (LANES, LANES), 1)

    @pl.when(j == 0)
    def _():
        pad = jnp.zeros((LANES - n_new, FOX_W), F32)
        knp = jnp.concatenate([kn_ref[...], pad], axis=0).astype(BF16)
        incl = jnp.where(ki <= ci, 1.0, 0.0).astype(BF16)
        cn_t = _dotx01(lfn_ref[...], incl)
        s = jnp.dot(knp, qbd_ref[...], preferred_element_type=F32) - _expand_heads(cn_t, n_new)
        s = jnp.where(ki <= ci % n_new, s, NEG_INF)
        s = jnp.where(ki < n_new, s, NEG_INF)
        s = jnp.where(ci < FOX_HEADS * n_new, s, NEG_INF)
        s_new[...] = s
        m_ref[...] = jnp.max(s, axis=0, keepdims=True)
        carry_ref[...] = jnp.zeros_like(carry_ref)

    @pl.when(j < n_groups)
    def _():
        g = n_groups - 1 - j
        later = jnp.where(ki > ci, 1.0, 0.0).astype(BF16)
        m = m_ref[...]
        for r in reversed(range(pg)):
            lf_t = lf_refs[r][...]
            tail_t = _dotx01(lf_t, later) + carry_ref[...]
            carry_ref[...] = carry_ref[...] + jnp.sum(lf_t, axis=1, keepdims=True)
            s = jnp.dot(k_refs[r][...].astype(BF16), qbd_ref[...], preferred_element_type=F32)
            s = s + _expand_heads(tail_t, n_new)
            off = pl.multiple_of((g * pg + r) * PAGE_SIZE, PAGE_SIZE)
            s_all[pl.ds(off, PAGE_SIZE), :] = s
            m = jnp.maximum(m, jnp.max(s, axis=0, keepdims=True))
        m_ref[...] = m

    @pl.when(j == n_groups)
    def _():
        pad = jnp.zeros((LANES - n_new, FOX_W), F32)
        vnp = jnp.concatenate([vn_ref[...], pad], axis=0).astype(BF16)
        p = jnp.exp(s_new[...] - m_ref[...])
        l_ref[...] = jnp.sum(p, axis=0, keepdims=True)
        acc_ref[...] = jnp.dot(p.T.astype(BF16), vnp, preferred_element_type=F32)

    @pl.when(j >= n_groups)
    def _():
        g = 2 * n_groups - 1 - j
        m = m_ref[...]
        l = l_ref[...]
        for r in reversed(range(pg)):
            off = pl.multiple_of((g * pg + r) * PAGE_SIZE, PAGE_SIZE)
            p = jnp.exp(s_all[pl.ds(off, PAGE_SIZE), :] - m)
            l = l + jnp.sum(p, axis=0, keepdims=True)
            acc_ref[...] += jnp.dot(p.T.astype(BF16), v_refs[r][...].astype(BF16), preferred_element_type=F32)
        l_ref[...] = l

    @pl.when(j == 2 * n_groups - 1)
    def _():
        l_col = jnp.broadcast_to(l_ref[...], (LANES, LANES)).T
        for h in range(FOX_HEADS):
            rows = slice(h * n_new, (h + 1) * n_new)
            cols = slice(h * FOX_HD, (h + 1) * FOX_HD)
            o_ref[:, cols] = acc_ref[rows, cols] / l_col[rows, :]


def _fox_sample(page_table, qbd, kn, vn, lfn_t, ck, cv, clf_t):
    b, n_pages = page_table.shape
    n_new = kn.shape[1]
    pg = 8 if n_pages % 8 == 0 else 1
    ng = n_pages // pg

    def kmap(r):
        return lambda bi, j, pt: (pt[bi, (ng - 1 - jnp.minimum(j, ng - 1)) * pg + r], 0, 0)

    def vmap_(r):
        return lambda bi, j, pt: (pt[bi, (ng - 1 - jnp.maximum(j - ng, 0)) * pg + r], 0, 0)

    per_b = lambda bi, j, pt: (bi, 0, 0)
    in_specs = [pl.BlockSpec((None, FOX_W, LANES), per_b),
                pl.BlockSpec((None, n_new, FOX_W), per_b),
                pl.BlockSpec((None, n_new, FOX_W), per_b),
                pl.BlockSpec((None, FOX_HEADS, LANES), per_b)]
    in_specs += [pl.BlockSpec((None, PAGE_SIZE, FOX_W), kmap(r)) for r in range(pg)]
    in_specs += [pl.BlockSpec((None, FOX_HEADS, PAGE_SIZE), kmap(r)) for r in range(pg)]
    in_specs += [pl.BlockSpec((None, PAGE_SIZE, FOX_W), vmap_(r)) for r in range(pg)]
    grid_spec = pltpu.PrefetchScalarGridSpec(
        num_scalar_prefetch=1,
        grid=(b, 2 * ng),
        in_specs=in_specs,
        out_specs=pl.BlockSpec((None, n_new, FOX_W), per_b),
        scratch_shapes=[pltpu.VMEM((n_pages * PAGE_SIZE, LANES), F32),
                        pltpu.VMEM((LANES, LANES), F32),
                        pltpu.VMEM((1, LANES), F32),
                        pltpu.VMEM((1, LANES), F32),
                        pltpu.VMEM((LANES, FOX_W), F32),
                        pltpu.VMEM((FOX_HEADS, 1), F32)],
    )
    return pl.pallas_call(
        functools.partial(_fox_sample_kernel, pages_per_step=pg, n_groups=ng),
        out_shape=jax.ShapeDtypeStruct((b, n_new, FOX_W), F32),
        grid_spec=grid_spec,
        compiler_params=_cparams(("parallel", "arbitrary")),
        name="fox_sample",
    )(page_table, qbd, kn, vn, lfn_t, *([ck] * pg), *([clf_t] * pg), *([cv] * pg))


def _gla_chunk(q, k, v, zs, f2p, fb, s_state, n_row_blocks, rb):
    c = q.shape[0]
    z = jnp.dot(zs.astype(BF16), f2p, preferred_element_type=F32) + fb
    la = _log_sigmoid(z) * (1.0 / GLA_GATE_TAU)
    ri = lax.broadcasted_iota(jnp.int32, (c, c), 0)
    ci = lax.broadcasted_iota(jnp.int32, (c, c), 1)
    b = _dot01(jnp.where(ci <= ri, 1.0, 0.0).astype(BF16), la)
    bl = b[c - 1:c]
    rbi = lax.broadcasted_iota(jnp.int32, (rb, c), 0)
    cbi = lax.broadcasted_iota(jnp.int32, (rb, c), 1)
    rows = []
    for i in range(n_row_blocks):
        r0 = i * rb
        qb, kb_, bb = q[r0:r0 + rb], k[r0:r0 + rb], b[r0:r0 + rb]
        a = jnp.zeros((rb, c), F32)
        if i > 0:
            bref = b[r0:r0 + 1]
            qt = qb * jnp.exp(bb - bref)
            kt = k * jnp.exp(jnp.minimum(bref - b, 0.0))
            a = lax.dot_general(qt.astype(BF16), kt.astype(BF16), NT, preferred_element_type=F32)
            a = jnp.where(cbi < r0, a, 0.0)
        for s in range(rb):
            e = jnp.exp(jnp.minimum(bb - bb[s:s + 1], 0.0))
            colv = jnp.sum(qb * kb_[s:s + 1] * e, axis=-1, keepdims=True)
            a = jnp.where(cbi == r0 + s, jnp.where(rbi >= s, colv, 0.0), a)
        rows.append(a)
    a = rows[0] if len(rows) == 1 else jnp.concatenate(rows, axis=0)
    nr = n_row_blocks * rb
    o = (jnp.dot(a.astype(BF16), v, preferred_element_type=F32)
         + jnp.dot((q[:nr] * jnp.exp(b[:nr])).astype(BF16), s_state.astype(BF16), preferred_element_type=F32))
    kh = (k * jnp.exp(bl - b)).astype(BF16)
    blc = jnp.exp(jnp.broadcast_to(bl, (GLA_DK, GLA_DK)).T)
    upd = lax.dot_general(kh, v, (((0,), (0,)), ((), ())), preferred_element_type=F32)
    s_next = jnp.concatenate([blc] * (GLA_DV // GLA_DK), axis=1) * s_state + upd
    return o, s_next


def _gla_out(o, gr, ong):
    return _rms(o, ong) * (gr * jax.nn.sigmoid(gr))


def _gla_prompt_kernel(q_ref, k_ref, v_ref, gr_ref, zs_ref, f2p_ref, fb_ref, ong_ref, s0_ref,
                       og_ref, sout_ref, s_ref):
    i = pl.program_id(1)
    tb = q_ref.shape[0]
    c = GLA_CHUNK

    @pl.when(i == 0)
    def _():
        s_ref[...] = s0_ref[...]

    def chunk(n, carry):
        r = pl.multiple_of(n * c, c)
        q = q_ref[pl.ds(r, c), :].astype(F32) * (GLA_DK ** -0.5)
        k = k_ref[pl.ds(r, c), :].astype(F32)
        o, s_next = _gla_chunk(q, k, v_ref[pl.ds(r, c), :], zs_ref[pl.ds(r, c), :], f2p_ref[...], fb_ref[...],
                               s_ref[...], c // GLA_ROWBLOCK, GLA_ROWBLOCK)
        s_ref[...] = s_next
        og_ref[pl.ds(r, c), :] = _gla_out(o, gr_ref[pl.ds(r, c), :].astype(F32), ong_ref[...]).astype(og_ref.dtype)
        return carry

    lax.fori_loop(0, tb // c, chunk, 0)

    @pl.when(i == pl.num_programs(1) - 1)
    def _():
        sout_ref[...] = s_ref[...]


def _gla_prompt(z1, zs, f2p, fb, ong, s0, tp):
    tb = _pick(tp, (512, 256, 128, 64))
    kq = GLA_QK_W // GLA_DK
    return pl.pallas_call(
        _gla_prompt_kernel,
        out_shape=(jax.ShapeDtypeStruct((tp, GLA_V_W), BF16),
                   jax.ShapeDtypeStruct((GLA_HEADS, GLA_DK, GLA_DV), F32)),
        grid=(GLA_HEADS, tp // tb),
        in_specs=[pl.BlockSpec((tb, GLA_DK), lambda h, i: (i, h)),
                  pl.BlockSpec((tb, GLA_DK), lambda h, i: (i, kq + h)),
                  pl.BlockSpec((tb, GLA_DV), lambda h, i: (i, kq + h)),
                  pl.BlockSpec((tb, GLA_DV), lambda h, i: (i, 2 * kq + h)),
                  pl.BlockSpec((tb, LANES), lambda h, i: (i, 0)),
                  pl.BlockSpec((None, LANES, GLA_DK), lambda h, i: (h, 0, 0)),
                  pl.BlockSpec((None, 1, GLA_DK), lambda h, i: (h, 0, 0)),
                  pl.BlockSpec((1, GLA_DV), lambda h, i: (0, 0)),
                  pl.BlockSpec((None, GLA_DK, GLA_DV), lambda h, i: (h, 0, 0))],
        out_specs=(pl.BlockSpec((tb, GLA_DV), lambda h, i: (i, h)),
                   pl.BlockSpec((None, GLA_DK, GLA_DV), lambda h, i: (h, 0, 0))),
        scratch_shapes=[pltpu.VMEM((GLA_DK, GLA_DV), F32)],
        compiler_params=_cparams(("parallel", "arbitrary")),
        name="gla_prompt",
    )(z1, z1, z1, z1, zs, f2p, fb, ong, s0)


def _gla_sample_kernel(q_ref, k_ref, v_ref, gr_ref, zs_ref, f2p_ref, fb_ref, ong_ref, s0_ref, og_ref, sout_ref):
    t = q_ref.shape[0]
    pad = lambda x: jnp.concatenate([x, jnp.zeros((LANES - t, x.shape[1]), x.dtype)], axis=0)
    q = pad(q_ref[...] * (GLA_DK ** -0.5))
    k = pad(k_ref[...])
    v = pad(v_ref[...]).astype(BF16)
    zs = pad(zs_ref[...])
    rowi = lax.broadcasted_iota(jnp.int32, (LANES, GLA_DK), 0)
    fb = jnp.where(rowi < t, fb_ref[...], 1e4)
    o, s_next = _gla_chunk(q, k, v, zs, f2p_ref[...], fb, s0_ref[...], 1, t)
    og_ref[...] = _gla_out(o, gr_ref[...], ong_ref[...])
    sout_ref[...] = s_next


def _gla_sample(z1s, zss, f2p, fb, ong, s0, n_new):
    ts = z1s.shape[0]
    b = ts // n_new
    kq = GLA_QK_W // GLA_DK
    return pl.pallas_call(
        _gla_sample_kernel,
        out_shape=(jax.ShapeDtypeStruct((ts, GLA_V_W), F32),
                   jax.ShapeDtypeStruct((b, GLA_HEADS, GLA_DK, GLA_DV), F32)),
        grid=(b, GLA_HEADS),
        in_specs=[pl.BlockSpec((n_new, GLA_DK), lambda bi, h: (bi, h)),
                  pl.BlockSpec((n_new, GLA_DK), lambda bi, h: (bi, kq + h)),
                  pl.BlockSpec((n_new, GLA_DV), lambda bi, h: (bi, kq + h)),
                  pl.BlockSpec((n_new, GLA_DV), lambda bi, h: (bi, 2 * kq + h)),
                  pl.BlockSpec((n_new, LANES), lambda bi, h: (bi, 0)),
                  pl.BlockSpec((None, LANES, GLA_DK), lambda bi, h: (h, 0, 0)),
                  pl.BlockSpec((None, 1, GLA_DK), lambda bi, h: (h, 0, 0)),
                  pl.BlockSpec((1, GLA_DV), lambda bi, h: (0, 0)),
                  pl.BlockSpec((None, None, GLA_DK, GLA_DV), lambda bi, h: (bi, h, 0, 0))],
        out_specs=(pl.BlockSpec((n_new, GLA_DV), lambda bi, h: (bi, h)),
                   pl.BlockSpec((None, None, GLA_DK, GLA_DV), lambda bi, h: (bi, h, 0, 0))),
        compiler_params=_cparams(("parallel", "parallel")),
        name="gla_sample",
    )(z1s, z1s, z1s, z1s, zss, f2p, fb, ong, s0)


def _merge_kernel(ogp_ref, ogs_ref, ofp_ref, ofs_ref, gate_ref, wpa_ref, wpb_ref, o_ref, *, n_prompt_blocks):
    i = pl.program_id(0)

    def run(og, of):
        d = o_ref.shape[1]
        ma = gate_ref[:, :d].astype(F32)
        mb = gate_ref[:, d:].astype(F32)
        a = jnp.dot(og, wpa_ref[...], preferred_element_type=F32)
        b = jnp.dot(of, wpb_ref[...], preferred_element_type=F32)
        o_ref[...] = (jax.nn.sigmoid(ma) * a + jax.nn.sigmoid(mb) * b).astype(o_ref.dtype)

    @pl.when(i < n_prompt_blocks)
    def _():
        run(ogp_ref[...], ofp_ref[...])

    @pl.when(i >= n_prompt_blocks)
    def _():
        run(ogs_ref[...], ofs_ref[...])


def _merge(ogp, ogs, ofp, ofs, z3, wpa, wpb):
    tp, ts = ogp.shape[0], ogs.shape[0]
    d = wpa.shape[1]
    tb = ROW_BLOCK
    npb, nb = tp // tb, (tp + ts) // tb
    pidx = lambda i: (jnp.minimum(i, npb - 1), 0)
    sidx = lambda i: (jnp.maximum(i - npb, 0), 0)
    const = lambda i: (0, 0)
    return pl.pallas_call(
        functools.partial(_merge_kernel, n_prompt_blocks=npb),
        out_shape=jax.ShapeDtypeStruct((tp + ts, d), BF16),
        grid=(nb,),
        in_specs=[pl.BlockSpec((tb, GLA_V_W), pidx), pl.BlockSpec((tb, GLA_V_W), sidx),
                  pl.BlockSpec((tb, FOX_W), pidx), pl.BlockSpec((tb, FOX_W), sidx),
                  pl.BlockSpec((tb, 2 * d), lambda i: (i, 0)),
                  pl.BlockSpec(wpa.shape, const), pl.BlockSpec(wpb.shape, const)],
        out_specs=pl.BlockSpec((tb, d), lambda i: (i, 0)),
        compiler_params=_cparams(("parallel",)),
        name="merge",
    )(ogp, ogs, ofp, ofs, z3, wpa, wpb)


def _outproj_kernel(m_ref, w_ref, xp_ref, xs_ref, g1p_ref, g1s_ref, n2g_ref, scp_ref, shp_ref, scs_ref, shs_ref,
                    x1_ref, h2_ref, *, n_prompt_blocks):
    i = pl.program_id(0)
    y = jnp.dot(m_ref[...], w_ref[...], preferred_element_type=F32)

    def run(x, g1, sc, sh):
        x1 = x + g1 * y
        x1_ref[...] = x1
        h2_ref[...] = (_rms(x1, n2g_ref[...]) * (1.0 + sc) + sh).astype(h2_ref.dtype)

    @pl.when(i < n_prompt_blocks)
    def _():
        run(xp_ref[...], g1p_ref[...], scp_ref[...], shp_ref[...])

    @pl.when(i >= n_prompt_blocks)
    def _():
        run(xs_ref[...], g1s_ref[...], scs_ref[...], shs_ref[...])


def _outproj(merged, wout, xp, xs, g1p, g1s, n2g, scp, shp, scs, shs):
    tp, d = xp.shape
    ts = xs.shape[0]
    tb = ROW_BLOCK
    npb, nb = tp // tb, (tp + ts) // tb
    pidx = lambda i: (jnp.minimum(i, npb - 1), 0)
    sidx = lambda i: (jnp.maximum(i - npb, 0), 0)
    const = lambda i: (0, 0)
    row = pl.BlockSpec((1, d), const)
    srow = pl.BlockSpec((tb, d), sidx)
    return pl.pallas_call(
        functools.partial(_outproj_kernel, n_prompt_blocks=npb),
        out_shape=(jax.ShapeDtypeStruct((tp + ts, d), F32), jax.ShapeDtypeStruct((tp + ts, d), BF16)),
        grid=(nb,),
        in_specs=[pl.BlockSpec((tb, d), lambda i: (i, 0)), pl.BlockSpec(wout.shape, const),
                  pl.BlockSpec((tb, d), pidx), srow, row, srow, row, row, row, srow, srow],
        out_specs=(pl.BlockSpec((tb, d), lambda i: (i, 0)), pl.BlockSpec((tb, d), lambda i: (i, 0))),
        compiler_params=_cparams(("parallel",)),
        name="out_proj",
    )(merged, wout, xp, xs, g1p, g1s, n2g, scp, shp, scs, shs)


def _top_rows(x, n):
    out = []
    for _ in range(n):
        m = jnp.max(x, axis=0, keepdims=True)
        out.append(m)
        x = jnp.where(x == m, NEG_INF, x)
    return out


def _peer_topk_kernel(q_ref, sk_ref, th_ref, ka_ref, s2_ref, e2_ref):
    for h in range(PEER_HEADS):
        s = []
        for p in range(2):
            c0 = (2 * h + p) * PEER_NKEYS
            s.append(lax.dot_general(sk_ref[2 * h + p], q_ref[:, c0:c0 + PEER_NKEYS], NT,
                                     preferred_element_type=F32))
        v1 = _top_rows(s[0], PEER_TOPK)
        v2 = _top_rows(s[1], PEER_TOPK)
        cands = [v1[a] + v2[b] for a in range(PEER_TOPK) for b in range(PEER_TOPK)
                 if (a + 1) * (b + 1) <= PEER_TOPK]
        cands += [jnp.full_like(v1[0], NEG_INF)] * (-len(cands) % 8)
        sel = _top_rows(jnp.concatenate(cands, axis=0), PEER_TOPK)
        zsum = sel[0] * 0.0 + 1.0
        for r in range(1, PEER_TOPK):
            zsum = zsum + jnp.exp(sel[r] - sel[0])
        th_ref[h] = sel[PEER_TOPK - 1] - s[0]
        ka_ref[h] = jnp.exp(s[0] - v1[0]) / zsum
        s2_ref[h] = s[1]
        e2_ref[h] = jnp.exp(s[1] - v2[0])


def _peer_topk(q, sk):
    t = q.shape[0]
    tb = ROW_BLOCK
    shp = jax.ShapeDtypeStruct((PEER_HEADS, PEER_NKEYS, t), F32)
    ospec = pl.BlockSpec((PEER_HEADS, PEER_NKEYS, tb), lambda i: (0, 0, i))
    return pl.pallas_call(
        _peer_topk_kernel,
        out_shape=(shp, shp, shp, shp),
        grid=(t // tb,),
        in_specs=[pl.BlockSpec((tb, q.shape[1]), lambda i: (i, 0)),
                  pl.BlockSpec(sk.shape, lambda i: (0, 0, 0))],
        out_specs=(ospec, ospec, ospec, ospec),
        compiler_params=_cparams(("parallel",)),
        name="peer_topk",
    )(q, sk)


def _peer_main_kernel(h2_ref, u_ref, v_ref, th_ref, ka_ref, s2_ref, e2_ref, o_ref, w_ref):
    j = pl.program_id(1)
    te = u_ref.shape[0]

    @pl.when(j == 0)
    def _():
        o_ref[...] = jnp.zeros_like(o_ref)

    at = lax.dot_general(u_ref[...], h2_ref[...], NT, preferred_element_type=F32)
    for a in range(te // PEER_NKEYS):
        x = at[a * PEER_NKEYS:(a + 1) * PEER_NKEYS]
        ge = 0.5 * x * (1.0 + lax.erf(x * (2.0 ** -0.5)))
        g = jnp.zeros_like(x)
        for h in range(PEER_HEADS):
            g = g + jnp.where(s2_ref[h] >= th_ref[h, a:a + 1, :], e2_ref[h] * ka_ref[h, a:a + 1, :], 0.0)
        w_ref[a * PEER_NKEYS:(a + 1) * PEER_NKEYS, :] = (g * ge).astype(w_ref.dtype)
    o_ref[...] += lax.dot_general(w_ref[...], v_ref[...], (((0,), (0,)), ((), ())), preferred_element_type=F32)


def _peer_main(h2, u, v, th, ka, s2, e2):
    t, d = h2.shape
    n_exp = u.shape[0]
    tm = _pick(t, (768, 512, 256))
    te = 1024
    a_per = te // PEER_NKEYS
    sel_spec = pl.BlockSpec((PEER_HEADS, a_per, tm), lambda i, j: (0, j, i))
    tok_spec = pl.BlockSpec((PEER_HEADS, PEER_NKEYS, tm), lambda i, j: (0, 0, i))
    return pl.pallas_call(
        _peer_main_kernel,
        out_shape=jax.ShapeDtypeStruct((t, d), F32),
        grid=(t // tm, n_exp // te),
        in_specs=[pl.BlockSpec((tm, d), lambda i, j: (i, 0)),
                  pl.BlockSpec((te, d), lambda i, j: (j, 0)),
                  pl.BlockSpec((te, d), lambda i, j: (j, 0)),
                  sel_spec, sel_spec, tok_spec, tok_spec],
        out_specs=pl.BlockSpec((tm, d), lambda i, j: (i, 0)),
        scratch_shapes=[pltpu.VMEM((te, tm), BF16)],
        compiler_params=_cparams(("parallel", "arbitrary")),
        name="peer_main",
    )(h2, u, v, th, ka, s2, e2)


def _final_kernel(x1_ref, y_ref, g2p_ref, g2s_ref, op_ref, os_ref, *, n_prompt_blocks):
    i = pl.program_id(0)

    @pl.when(i < n_prompt_blocks)
    def _():
        op_ref[...] = x1_ref[...] + g2p_ref[...] * y_ref[...]

    @pl.when(i >= n_prompt_blocks)
    def _():
        os_ref[...] = x1_ref[...] + g2s_ref[...] * y_ref[...]


def _final(x1, y, g2p, g2s, tp):
    t, d = x1.shape
    ts = t - tp
    tb = ROW_BLOCK
    npb, nb = tp // tb, t // tb
    pidx = lambda i: (jnp.minimum(i, npb - 1), 0)
    sidx = lambda i: (jnp.maximum(i - npb, 0), 0)
    return pl.pallas_call(
        functools.partial(_final_kernel, n_prompt_blocks=npb),
        out_shape=(jax.ShapeDtypeStruct((tp, d), F32), jax.ShapeDtypeStruct((ts, d), F32)),
        grid=(nb,),
        in_specs=[pl.BlockSpec((tb, d), lambda i: (i, 0)), pl.BlockSpec((tb, d), lambda i: (i, 0)),
                  pl.BlockSpec((1, d), lambda i: (0, 0)), pl.BlockSpec((tb, d), sidx)],
        out_specs=(pl.BlockSpec((tb, d), pidx), pl.BlockSpec((tb, d), sidx)),
        compiler_params=_cparams(("arbitrary",)),
        name="final_residual",
    )(x1, y, g2p, g2s)


def _layer(xp, xs, c_prompt, c_sample, state_gla, cache_k, cache_v, cache_logf, page_table,
           norm1_g, norm2_g, w_ada, b_ada, w_in, gla_f2, gla_fb, gla_on_g, fox_qn_g, fox_kn_g, fox_fb,
           w_pa, w_pb, w_out, peer_wq, peer_subkeys, peer_u, peer_v):
    bp, tpp, d = xp.shape
    assert bp == 1 and d == D_MODEL
    b, n_new, _ = xs.shape
    tp, ts = bp * tpp, b * n_new
    assert tp % ROW_BLOCK == 0 and ts % ROW_BLOCK == 0
    xp2 = xp.reshape(tp, d)
    xs2 = xs.reshape(ts, d)

    n_c = 1 + b
    n_c_pad = -(-n_c // 8) * 8
    c_all = jnp.concatenate([c_prompt, c_sample, jnp.zeros((n_c_pad - n_c, d), F32)], axis=0)
    mod = _ada(c_all, w_ada, b_ada)
    mod_p = [mod[0:1, i * d:(i + 1) * d] for i in range(6)]
    mod_s = [jnp.repeat(mod[1:n_c, i * d:(i + 1) * d], n_new, axis=0) for i in range(6)]
    sh1p, sc1p, g1p, sh2p, sc2p, g2p = mod_p
    sh1s, sc1s, g1s, sh2s, sc2s, g2s = mod_s

    h = _normmod(xp2, xs2, norm1_g.reshape(1, d), sc1p, sh1p, sc1s, sh1s)

    w1 = jnp.concatenate([w_in[:, :_OFF_GF], w_in[:, _OFF_FQ:_OFF_FK]], axis=1).astype(BF16)
    w2 = w_in[:, _OFF_FK:_OFF_FF].astype(BF16)
    w3 = w_in[:, _OFF_MA:_OFF_END].astype(BF16)
    w4 = jnp.concatenate([w_in[:, _OFF_FF:_OFF_MA], w_in[:, _OFF_GF:_OFF_FQ],
                          jnp.zeros((d, LANES - FOX_HEADS - GLA_GATE_RANK), F32)], axis=1).astype(BF16)
    z1 = _matmul(h, w1, BF16, "in_proj_mixers")
    z2 = _matmul(h, w2, F32, "in_proj_fox_kv")
    z3 = _matmul(h, w3, BF16, "in_proj_gates")
    zs = _matmul(h, w4, F32, "in_proj_small")

    fb_pad = jnp.concatenate([fox_fb, jnp.zeros((LANES - FOX_HEADS,), F32)]).reshape(1, LANES)
    (qn, kb, vb, cum, k_p, v_p, lf_p, k_s, v_s, lf_s) = _fox_prep(
        z1, z2, zs, fox_qn_g.reshape(1, FOX_HD), fox_kn_g.reshape(1, FOX_HD), fb_pad, tp, n_new)
    neg_cum_t = (-cum[:tp, :FOX_HEADS]).T.reshape(FOX_HEADS, 1, tp)
    of_p = _fox_prompt(qn, kb, vb, neg_cum_t, tp)

    qs = qn[tp:].reshape(b, n_new, FOX_HEADS, FOX_HD)
    eye = jnp.eye(FOX_HEADS, dtype=BF16)
    qbd = (qs.transpose(0, 2, 3, 1)[:, :, :, None, :] * eye[None, :, None, :, None])
    qbd = qbd.reshape(b, FOX_W, FOX_HEADS * n_new)
    qbd = jnp.concatenate([qbd, jnp.zeros((b, FOX_W, LANES - FOX_HEADS * n_new), BF16)], axis=2)
    lfn_t = lf_s[:, :FOX_HEADS].reshape(b, n_new, FOX_HEADS).transpose(0, 2, 1)
    lfn_t = jnp.concatenate([lfn_t, jnp.zeros((b, FOX_HEADS, LANES - n_new), F32)], axis=2)
    n_pool = cache_k.shape[0]
    ck = cache_k.reshape(n_pool, PAGE_SIZE, FOX_W)
    cv = cache_v.reshape(n_pool, PAGE_SIZE, FOX_W)
    clf_t = cache_logf.transpose(0, 2, 1)
    of_s = _fox_sample(page_table, qbd, k_s.reshape(b, n_new, FOX_W), v_s.reshape(b, n_new, FOX_W), lfn_t,
                       ck, cv, clf_t).reshape(ts, FOX_W).astype(BF16)

    f2p = jnp.zeros((LANES, GLA_QK_W), F32).at[FOX_HEADS:FOX_HEADS + GLA_GATE_RANK].set(gla_f2)
    f2p = f2p.reshape(LANES, GLA_HEADS, GLA_DK).transpose(1, 0, 2).astype(BF16)
    fbh = gla_fb.reshape(GLA_HEADS, 1, GLA_DK)
    ong = gla_on_g.reshape(1, GLA_DV)
    og_p, sg_p = _gla_prompt(z1, zs, f2p, fbh, ong, jnp.zeros((GLA_HEADS, GLA_DK, GLA_DV), F32), tp)
    og_s, sg_s = _gla_sample(z1[tp:].astype(F32), zs[tp:], f2p, fbh, ong, state_gla, n_new)

    merged = _merge(og_p, og_s.astype(BF16), of_p, of_s, z3, w_pa.astype(BF16), w_pb.astype(BF16))
    x1, h2 = _outproj(merged, w_out.astype(BF16), xp2, xs2, g1p, g1s, norm2_g.reshape(1, d),
                      sc2p, sh2p, sc2s, sh2s)

    q = _matmul(h2, peer_wq.astype(BF16), BF16, "peer_query")
    sk = peer_subkeys.reshape(PEER_HEADS * 2, PEER_NKEYS, -1).astype(BF16)
    th, ka, s2, e2 = _peer_topk(q, sk)
    y = _peer_main(h2, peer_u.astype(BF16), peer_v.astype(BF16), th, ka, s2, e2)
    out_p, out_s = _final(x1, y, g2p, g2s, tp)

    return (out_p.reshape(bp, tpp, d), out_s.reshape(b, n_new, d),
            sg_p.reshape(bp, GLA_HEADS, GLA_DK, GLA_DV), sg_s,
            k_p.reshape(bp, tpp, FOX_HEADS, FOX_HD), v_p.reshape(bp, tpp, FOX_HEADS, FOX_HD),
            lf_p[:, :FOX_HEADS].reshape(bp, tpp, FOX_HEADS),
            k_s.reshape(b, n_new, FOX_HEADS, FOX_HD), v_s.reshape(b, n_new, FOX_HEADS, FOX_HD),
            lf_s[:, :FOX_HEADS].reshape(b, n_new, FOX_HEADS))


def kernel(x_prompt, x_sample, c_prompt, c_sample, state_gla, cache_k, cache_v, cache_logf, page_table,
           norm1_g, norm2_g, w_ada, b_ada, w_in, gla_f2, gla_fb, gla_on_g, fox_qn_g, fox_kn_g, fox_fb,
           w_pa, w_pb, w_out, peer_wq, peer_subkeys, peer_u, peer_v):
    depth = w_in.shape[0]
    xp, xs = x_prompt, x_sample
    outs = []
    for l in range(depth):
        res = _layer(xp, xs, c_prompt, c_sample, state_gla[l], cache_k[l], cache_v[l], cache_logf[l], page_table,
                     norm1_g[l], norm2_g[l], w_ada[l], b_ada[l], w_in[l], gla_f2[l], gla_fb[l], gla_on_g[l],
                     fox_qn_g[l], fox_kn_g[l], fox_fb[l], w_pa[l], w_pb[l], w_out[l],
                     peer_wq[l], peer_subkeys[l], peer_u[l], peer_v[l])
        xp, xs = res[0], res[1]
        outs.append(res[2:])
    stacked = [jnp.stack([o[i] for o in outs]) for i in range(8)]
    return (xp, xs, *stacked)
```

```python
import functools

import jax
import jax.numpy as jnp
from jax import lax
from jax.experimental import pallas as pl
from jax.experimental.pallas import tpu as pltpu

F32 = jnp.float32
BF16 = jnp.bfloat16

D_MODEL = 2048
GLA_HEADS = 4
GLA_DK = 128
GLA_DV = 256
GLA_GATE_RANK = 16
GLA_GATE_TAU = 16.0
GLA_CHUNK = 64
GLA_ROWBLOCK = 16
FOX_HEADS = 8
FOX_HD = 128
FOX_W = FOX_HEADS * FOX_HD
PEER_HEADS = 8
PEER_NKEYS = 128
PEER_TOPK = 16
PAGE_SIZE = 128
EPS = 1e-6

GLA_QK_W = GLA_HEADS * GLA_DK
GLA_V_W = GLA_HEADS * GLA_DV
_OFF_GF = 2 * GLA_QK_W + 2 * GLA_V_W
_OFF_FQ = _OFF_GF + GLA_GATE_RANK
_OFF_FK = _OFF_FQ + FOX_W
_OFF_FV = _OFF_FK + FOX_W
_OFF_FF = _OFF_FV + FOX_W
_OFF_MA = _OFF_FF + FOX_HEADS
_OFF_MB = _OFF_MA + D_MODEL
_OFF_END = _OFF_MB + D_MODEL

LANES = 128
ROW_BLOCK = 256
VMEM_LIMIT = 56 * 1024 * 1024

NT = (((1,), (1,)), ((), ()))
NEG_INF = float("-inf")
LOG2E = 1.4426950408889634


def _cparams(sem):
    return pltpu.CompilerParams(dimension_semantics=sem, vmem_limit_bytes=VMEM_LIMIT)


def _split3(x):
    hi = x.astype(BF16)
    r = x - hi.astype(F32)
    mid = r.astype(BF16)
    lo = (r - mid.astype(F32)).astype(BF16)
    return hi, mid, lo


def _dot01(m01, x):
    acc = None
    for p in _split3(x):
        t = jnp.dot(m01, p, preferred_element_type=F32)
        acc = t if acc is None else acc + t
    return acc


def _dotx01(x, m01):
    acc = None
    for p in _split3(x):
        t = jnp.dot(p, m01, preferred_element_type=F32)
        acc = t if acc is None else acc + t
    return acc


def _log_sigmoid(z):
    return jnp.minimum(z, 0.0) - jnp.log1p(jnp.exp(-jnp.abs(z)))


def _rms(x, g):
    return x * lax.rsqrt(jnp.mean(x * x, axis=-1, keepdims=True) + EPS) * g


def _ada_kernel(c_ref, w_ref, b_ref, o_ref):
    o_ref[...] = jnp.dot(c_ref[...].astype(BF16), w_ref[...].astype(BF16),
                         preferred_element_type=F32) + b_ref[...]


def _ada(c_all, w_ada, b_ada):
    m, k = c_all.shape
    n = w_ada.shape[1]
    tn = 1024
    return pl.pallas_call(
        _ada_kernel,
        out_shape=jax.ShapeDtypeStruct((m, n), F32),
        grid=(n // tn,),
        in_specs=[pl.BlockSpec((m, k), lambda j: (0, 0)),
                  pl.BlockSpec((k, tn), lambda j: (0, j)),
                  pl.BlockSpec((1, tn), lambda j: (0, j))],
        out_specs=pl.BlockSpec((m, tn), lambda j: (0, j)),
        compiler_params=_cparams(("parallel",)),
        name="ada_mod",
    )(c_all, w_ada, b_ada.reshape(1, n))


def _normmod_kernel(xp_ref, xs_ref, g_ref, scp_ref, shp_ref, scs_ref, shs_ref, o_ref, *, n_prompt_blocks):
    i = pl.program_id(0)

    @pl.when(i < n_prompt_blocks)
    def _():
        h = _rms(xp_ref[...], g_ref[...]) * (1.0 + scp_ref[...]) + shp_ref[...]
        o_ref[...] = h.astype(o_ref.dtype)

    @pl.when(i >= n_prompt_blocks)
    def _():
        h = _rms(xs_ref[...], g_ref[...]) * (1.0 + scs_ref[...]) + shs_ref[...]
        o_ref[...] = h.astype(o_ref.dtype)


def _normmod(xp, xs, g, scp, shp, scs, shs):
    tp, d = xp.shape
    ts = xs.shape[0]
    tb = ROW_BLOCK
    npb, nsb = tp // tb, ts // tb
    pidx = lambda i: (jnp.minimum(i, npb - 1), 0)
    sidx = lambda i: (jnp.maximum(i - npb, 0), 0)
    row = pl.BlockSpec((1, d), lambda i: (0, 0))
    return pl.pallas_call(
        functools.partial(_normmod_kernel, n_prompt_blocks=npb),
        out_shape=jax.ShapeDtypeStruct((tp + ts, d), BF16),
        grid=(npb + nsb,),
        in_specs=[pl.BlockSpec((tb, d), pidx), pl.BlockSpec((tb, d), sidx), row, row, row,
                  pl.BlockSpec((tb, d), sidx), pl.BlockSpec((tb, d), sidx)],
        out_specs=pl.BlockSpec((tb, d), lambda i: (i, 0)),
        compiler_params=_cparams(("parallel",)),
        name="norm_mod",
    )(xp, xs, g, scp, shp, scs, shs)


def _mm_kernel(a_ref, w_ref, o_ref):
    o_ref[...] = jnp.dot(a_ref[...], w_ref[...], preferred_element_type=F32).astype(o_ref.dtype)


def _pick(n, cands):
    for c in cands:
        if n % c == 0:
            return c
    return n


def _matmul(a, w, out_dtype, name):
    m, k = a.shape
    n = w.shape[1]
    tm = _pick(m, (1024, 768, 512, 256))
    tn = _pick(n, (1024, 512, 256, 128))
    return pl.pallas_call(
        _mm_kernel,
        out_shape=jax.ShapeDtypeStruct((m, n), out_dtype),
        grid=(m // tm, n // tn),
        in_specs=[pl.BlockSpec((tm, k), lambda i, j: (i, 0)),
                  pl.BlockSpec((k, tn), lambda i, j: (0, j))],
        out_specs=pl.BlockSpec((tm, tn), lambda i, j: (i, j)),
        compiler_params=_cparams(("parallel", "parallel")),
        name=name,
    )(a, w)


def _fox_prep_kernel(fq_ref, fkv_ref, zs_ref, qg_ref, kg_ref, fb_ref,
                     qn_ref, kb_ref, vb_ref, cum_ref, kp_ref, vp_ref, lfp_ref, ks_ref, vs_ref, lfs_ref,
                     carry_ref, *, n_prompt_blocks, seg_len):
    i = pl.program_id(0)
    tb = fq_ref.shape[0]
    is_prompt = i < n_prompt_blocks

    @pl.when(i == 0)
    def _():
        carry_ref[...] = jnp.zeros_like(carry_ref)

    scale = FOX_HD ** -0.5 * jnp.where(is_prompt, LOG2E, 1.0).astype(F32)
    fq = fq_ref[...].astype(F32)
    fkv = fkv_ref[...]
    kn = []
    for h in range(FOX_HEADS):
        sl = slice(h * FOX_HD, (h + 1) * FOX_HD)
        qn_ref[:, sl] = (_rms(fq[:, sl], qg_ref[...]) * scale).astype(qn_ref.dtype)
        kn.append(_rms(fkv[:, sl], kg_ref[...]))
    kn = jnp.concatenate(kn, axis=1)
    v = fkv[:, FOX_W:]
    kb_ref[...] = kn.astype(kb_ref.dtype)
    vb_ref[...] = v.astype(vb_ref.dtype)

    lf = _log_sigmoid(zs_ref[...] + fb_ref[...])
    row = lax.broadcasted_iota(jnp.int32, (tb, tb), 0)
    col = lax.broadcasted_iota(jnp.int32, (tb, tb), 1)

    @pl.when(is_prompt)
    def _():
        tri = jnp.where(col <= row, 1.0, 0.0).astype(BF16)
        cum = _dot01(tri, lf) + carry_ref[...]
        carry_ref[...] = cum[tb - 1:tb, :]
        cum_ref[...] = cum
        kp_ref[...] = kn
        vp_ref[...] = v
        lfp_ref[...] = lf

    @pl.when(jnp.logical_not(is_prompt))
    def _():
        tri = jnp.where(col <= row, jnp.where((row // seg_len) == (col // seg_len), 1.0, 0.0), 0.0).astype(BF16)
        cum_ref[...] = _dot01(tri, lf)
        ks_ref[...] = kn
        vs_ref[...] = v
        lfs_ref[...] = lf


def _fox_prep(z1, z2, zs, qg, kg, fb_pad, tp, seg_len):
    t_all = z1.shape[0]
    ts = t_all - tp
    tb = ROW_BLOCK
    npb, nb = tp // tb, t_all // tb
    pidx = lambda i: (jnp.minimum(i, npb - 1), 0)
    sidx = lambda i: (jnp.maximum(i - npb, 0), 0)
    allidx = lambda i: (i, 0)
    row = pl.BlockSpec((1, LANES), lambda i: (0, 0))
    fq_block = (_OFF_GF) // FOX_W
    return pl.pallas_call(
        functools.partial(_fox_prep_kernel, n_prompt_blocks=npb, seg_len=seg_len),
        out_shape=(jax.ShapeDtypeStruct((t_all, FOX_W), BF16),
                   jax.ShapeDtypeStruct((t_all, FOX_W), BF16),
                   jax.ShapeDtypeStruct((t_all, FOX_W), BF16),
                   jax.ShapeDtypeStruct((t_all, LANES), F32),
                   jax.ShapeDtypeStruct((tp, FOX_W), F32),
                   jax.ShapeDtypeStruct((tp, FOX_W), F32),
                   jax.ShapeDtypeStruct((tp, LANES), F32),
                   jax.ShapeDtypeStruct((ts, FOX_W), F32),
                   jax.ShapeDtypeStruct((ts, FOX_W), F32),
                   jax.ShapeDtypeStruct((ts, LANES), F32)),
        grid=(nb,),
        in_specs=[pl.BlockSpec((tb, FOX_W), lambda i: (i, fq_block)),
                  pl.BlockSpec((tb, 2 * FOX_W), allidx),
                  pl.BlockSpec((tb, LANES), allidx), row, row, row],
        out_specs=(pl.BlockSpec((tb, FOX_W), allidx), pl.BlockSpec((tb, FOX_W), allidx),
                   pl.BlockSpec((tb, FOX_W), allidx), pl.BlockSpec((tb, LANES), allidx),
                   pl.BlockSpec((tb, FOX_W), pidx), pl.BlockSpec((tb, FOX_W), pidx),
                   pl.BlockSpec((tb, LANES), pidx),
                   pl.BlockSpec((tb, FOX_W), sidx), pl.BlockSpec((tb, FOX_W), sidx),
                   pl.BlockSpec((tb, LANES), sidx)),
        scratch_shapes=[pltpu.VMEM((1, LANES), F32)],
        compiler_params=_cparams(("arbitrary",)),
        name="fox_prep",
    )(z1, z2, zs, qg, kg, fb_pad)


def _fox_prompt_kernel(q_ref, k_ref, v_ref, nc_ref, o_ref, *, tk):
    qi = pl.program_id(1)
    tq = q_ref.shape[0]
    q = q_ref[...]
    n_full = (qi * tq) // tk

    def block(c0, masked, carry):
        m, l, acc = carry
        s = lax.dot_general(q, k_ref[pl.ds(c0, tk), :], NT, preferred_element_type=F32) + nc_ref[:, pl.ds(c0, tk)]
        if masked:
            row = qi * tq + lax.broadcasted_iota(jnp.int32, (tq, tk), 0)
            col = c0 + lax.broadcasted_iota(jnp.int32, (tq, tk), 1)
            s = jnp.where(col <= row, s, NEG_INF)
        m_new = jnp.maximum(m, jnp.max(s, axis=1, keepdims=True))
        alpha = jnp.exp2(m - m_new)
        p = jnp.exp2(s - m_new)
        l = alpha * l + jnp.sum(p, axis=1, keepdims=True)
        acc = alpha * acc + jnp.dot(p.astype(BF16), v_ref[pl.ds(c0, tk), :], preferred_element_type=F32)
        return m_new, l, acc

    init = (jnp.full((tq, 1), NEG_INF, F32), jnp.zeros((tq, 1), F32), jnp.zeros((tq, FOX_HD), F32))
    carry = lax.fori_loop(0, n_full, lambda i, c: block(pl.multiple_of(i * tk, tk), False, c), init)
    _, l, acc = block(pl.multiple_of(n_full * tk, tk), True, carry)
    o_ref[...] = (acc / l).astype(o_ref.dtype)


def _fox_prompt(qn, kb, vb, neg_cum_t, tp):
    tk = _pick(tp, (512, 256, 128))
    tq = max(tk // 2, 128)
    return pl.pallas_call(
        functools.partial(_fox_prompt_kernel, tk=tk),
        out_shape=jax.ShapeDtypeStruct((tp, FOX_W), BF16),
        grid=(FOX_HEADS, tp // tq),
        in_specs=[pl.BlockSpec((tq, FOX_HD), lambda h, i: (i, h)),
                  pl.BlockSpec((tp, FOX_HD), lambda h, i: (0, h)),
                  pl.BlockSpec((tp, FOX_HD), lambda h, i: (0, h)),
                  pl.BlockSpec((None, 1, tp), lambda h, i: (h, 0, 0))],
        out_specs=pl.BlockSpec((tq, FOX_HD), lambda h, i: (i, h)),
        compiler_params=_cparams(("parallel", "parallel")),
        name="fox_prompt",
    )(qn, kb, vb, neg_cum_t)


def _expand_heads(x_t, n_q):
    hi, mid, lo = _split3(x_t)
    n = x_t.shape[0]
    stacked = jnp.concatenate([hi.astype(F32), mid.astype(F32), lo.astype(F32),
                               jnp.zeros((LANES - 3 * n, LANES), F32)], axis=0)
    k = lax.broadcasted_iota(jnp.int32, (LANES, LANES), 0)
    c = lax.broadcasted_iota(jnp.int32, (LANES, LANES), 1)
    e3 = jnp.where(k < 3 * n, jnp.where(c < n * n_q, jnp.where((k % n) == (c // n_q), 1.0, 0.0), 0.0), 0.0)
    return jnp.dot(stacked.T.astype(BF16), e3.astype(BF16), preferred_element_type=F32)


def _page_to_rows(page_ref, buf_ref, p):
    for h in range(FOX_HEADS):
        x = page_ref[pl.ds(h, PAGE_SIZE, stride=FOX_HEADS), :]
        buf_ref[p * PAGE_SIZE:(p + 1) * PAGE_SIZE, h * FOX_HD:(h + 1) * FOX_HD] = x.astype(BF16)


def _fox_sample_kernel(pt_ref, qbd_ref, kn_ref, vn_ref, lfn_ref, *rest, pages_per_step, n_groups):
    pg = pages_per_step
    k_refs = rest[:pg]
    lf_refs = rest[pg:2 * pg]
    v_refs = rest[2 * pg:3 * pg]
    o_ref = rest[3 * pg]
    s_all, s_new, m_ref, l_ref, acc_ref, carry_ref, kv_buf, xin_ref, eall_ref = rest[3 * pg + 1:]
    j = pl.program_id(1)
    n_new = kn_ref.shape[0]
    hp = LANES // pg
    rows = pg * PAGE_SIZE
    ki = lax.broadcasted_iota(jnp.int32, (LANES, LANES), 0)
    ci = lax.broadcasted_iota(jnp.int32, (LANES, LANES), 1)

    @pl.when(j == 0)
    def _():
        pad = jnp.zeros((LANES - n_new, FOX_W), F32)
        knp = jnp.concatenate([kn_ref[...], pad], axis=0).astype(BF16)
        incl = jnp.where(ki <= ci, 1.0, 0.0).astype(BF16)
        cn_t = _dotx01(lfn_ref[...], incl)
        s = jnp.dot(knp, qbd_ref[...], preferred_element_type=F32) - _expand_heads(cn_t, n_new)
        s = jnp.where(ki <= ci % n_new, s, NEG_INF)
        s = jnp.where(ki < n_new, s, NEG_INF)
        s = jnp.where(ci < FOX_HEADS * n_new, s, NEG_INF)
        s_new[...] = s
        m_ref[...] = jnp.max(s, axis=0, keepdims=True)
        carry_ref[...] = jnp.zeros_like(carry_ref)
        xin_ref[...] = jnp.zeros_like(xin_ref)
        kk = lax.broadcasted_iota(jnp.int32, eall_ref.shape, 0)
        cc = lax.broadcasted_iota(jnp.int32, eall_ref.shape, 1)
        hit = kk == (cc // LANES) * hp + (cc % LANES) // n_new
        eall_ref[...] = jnp.where(hit, jnp.where(cc % LANES < FOX_HEADS * n_new, 1.0, 0.0), 0.0).astype(BF16)

    @pl.when(j < n_groups)
    def _():
        g = n_groups - 1 - j
        for p in range(pg):
            _page_to_rows(k_refs[p], kv_buf, p)
            xin_ref[:, p * hp:p * hp + FOX_HEADS] = lf_refs[p][...]
        s = jnp.dot(kv_buf[...], qbd_ref[...], preferred_element_type=F32)
        x = xin_ref[...]
        later_t = jnp.where(ci > ki, 1.0, 0.0).astype(BF16)
        same_head = (ki % hp) == (ci % hp)
        later_page = jnp.where(same_head, jnp.where(ki // hp > ci // hp, 1.0, 0.0), 0.0).astype(BF16)
        any_page = jnp.where(same_head, 1.0, 0.0).astype(BF16)
        tot = jnp.broadcast_to(jnp.sum(x, axis=0, keepdims=True), (8, LANES))
        tail = _dot01(later_t, x) + (_dotx01(tot, later_page)[0:1] + carry_ref[...])
        carry_ref[...] = carry_ref[...] + _dotx01(tot, any_page)[0:1]
        ex = jnp.dot(jnp.concatenate(list(_split3(tail)), axis=0), eall_ref[...], preferred_element_type=F32)
        blocks = []
        for p in range(pg):
            cols = slice(p * LANES, (p + 1) * LANES)
            add = ex[0:LANES, cols] + ex[LANES:2 * LANES, cols] + ex[2 * LANES:3 * LANES, cols]
            blocks.append(s[p * PAGE_SIZE:(p + 1) * PAGE_SIZE] + add)
        s = jnp.concatenate(blocks, axis=0)
        s_all[pl.ds(pl.multiple_of(g * rows, rows), rows), :] = s
        m_ref[...] = jnp.maximum(m_ref[...], jnp.max(s, axis=0, keepdims=True))

    @pl.when(j == n_groups)
    def _():
        pad = jnp.zeros((LANES - n_new, FOX_W), F32)
        vnp = jnp.concatenate([vn_ref[...], pad], axis=0).astype(BF16)
        p = jnp.exp(s_new[...] - m_ref[...])
        l_ref[...] = jnp.sum(p, axis=0, keepdims=True)
        acc_ref[...] = jnp.dot(p.T.astype(BF16), vnp, preferred_element_type=F32)

    @pl.when(j >= n_groups)
    def _():
        g = 2 * n_groups - 1 - j
        for p in range(pg):
            _page_to_rows(v_refs[p], kv_buf, p)
        pexp = jnp.exp(s_all[pl.ds(pl.multiple_of(g * rows, rows), rows), :] - m_ref[...])
        l_ref[...] += jnp.sum(pexp, axis=0, keepdims=True)
        acc_ref[...] += jnp.dot(pexp.T.astype(BF16), kv_buf[...], preferred_element_type=F32)

    @pl.when(j == 2 * n_groups - 1)
    def _():
        l_col = jnp.broadcast_to(l_ref[...], (LANES, LANES)).T
        for h in range(FOX_HEADS):
            rows = slice(h * n_new, (h + 1) * n_new)
            cols = slice(h * FOX_HD, (h + 1) * FOX_HD)
            o_ref[:, cols] = acc_ref[rows, cols] / l_col[rows, :]


def _fox_sample(layer, page_table, qbd, kn, vn, lfn_t, ck, cv, clf):
    ck = ck.reshape(ck.shape[0], ck.shape[1], PAGE_SIZE * FOX_HEADS, FOX_HD)
    cv = cv.reshape(cv.shape[0], cv.shape[1], PAGE_SIZE * FOX_HEADS, FOX_HD)
    b, n_pages = page_table.shape
    n_new = kn.shape[1]
    pg = 8 if n_pages % 8 == 0 else 1
    ng = n_pages // pg
    assert FOX_HEADS * pg <= LANES and FOX_HEADS * n_new <= LANES

    def page_map(r, phase_v, n_trailing):
        def index(bi, j, pt):
            step = jnp.maximum(j - ng, 0) if phase_v else jnp.minimum(j, ng - 1)
            return (layer, pt[bi, (ng - 1 - step) * pg + r]) + (0,) * n_trailing
        return index

    per_b = lambda bi, j, pt: (bi, 0, 0)
    page_blk = (None, None, PAGE_SIZE * FOX_HEADS, FOX_HD)
    in_specs = [pl.BlockSpec((None, FOX_W, LANES), per_b),
                pl.BlockSpec((None, n_new, FOX_W), per_b),
                pl.BlockSpec((None, n_new, FOX_W), per_b),
                pl.BlockSpec((None, FOX_HEADS, LANES), per_b)]
    in_specs += [pl.BlockSpec(page_blk, page_map(r, False, 2)) for r in range(pg)]
    in_specs += [pl.BlockSpec((None, None, PAGE_SIZE, FOX_HEADS), page_map(r, False, 2)) for r in range(pg)]
    in_specs += [pl.BlockSpec(page_blk, page_map(r, True, 2)) for r in range(pg)]
    grid_spec = pltpu.PrefetchScalarGridSpec(
        num_scalar_prefetch=1,
        grid=(b, 2 * ng),
        in_specs=in_specs,
        out_specs=pl.BlockSpec((None, n_new, FOX_W), per_b),
        scratch_shapes=[pltpu.VMEM((n_pages * PAGE_SIZE, LANES), F32),
                        pltpu.VMEM((LANES, LANES), F32),
                        pltpu.VMEM((1, LANES), F32),
                        pltpu.VMEM((1, LANES), F32),
                        pltpu.VMEM((LANES, FOX_W), F32),
                        pltpu.VMEM((1, LANES), F32),
                        pltpu.VMEM((pg * PAGE_SIZE, FOX_W), BF16),
                        pltpu.VMEM((PAGE_SIZE, LANES), F32),
                        pltpu.VMEM((LANES, pg * LANES), BF16)],
    )
    return pl.pallas_call(
        functools.partial(_fox_sample_kernel, pages_per_step=pg, n_groups=ng),
        out_shape=jax.ShapeDtypeStruct((b, n_new, FOX_W), F32),
        grid_spec=grid_spec,
        compiler_params=_cparams(("parallel", "arbitrary")),
        name="fox_sample",
    )(page_table, qbd, kn, vn, lfn_t, *([ck] * pg), *([clf] * pg), *([cv] * pg))


def _gla_chunk(q, k, v, zs, f2p, fb, s_state, n_row_blocks, rb):
    c = q.shape[0]
    z = jnp.dot(zs.astype(BF16), f2p, preferred_element_type=F32) + fb
    la = _log_sigmoid(z) * (1.0 / GLA_GATE_TAU)
    ri = lax.broadcasted_iota(jnp.int32, (c, c), 0)
    ci = lax.broadcasted_iota(jnp.int32, (c, c), 1)
    b = _dot01(jnp.where(ci <= ri, 1.0, 0.0).astype(BF16), la)
    bl = b[c - 1:c]
    rbi = lax.broadcasted_iota(jnp.int32, (rb, c), 0)
    cbi = lax.broadcasted_iota(jnp.int32, (rb, c), 1)
    rows = []
    for i in range(n_row_blocks):
        r0 = i * rb
        qb, kb_, bb = q[r0:r0 + rb], k[r0:r0 + rb], b[r0:r0 + rb]
        a = jnp.zeros((rb, c), F32)
        if i > 0:
            bref = b[r0:r0 + 1]
            qt = qb * jnp.exp(bb - bref)
            kt = k * jnp.exp(jnp.minimum(bref - b, 0.0))
            a = lax.dot_general(qt.astype(BF16), kt.astype(BF16), NT, preferred_element_type=F32)
            a = jnp.where(cbi < r0, a, 0.0)
        for s in range(rb):
            e = jnp.exp(jnp.minimum(bb - bb[s:s + 1], 0.0))
            colv = jnp.sum(qb * kb_[s:s + 1] * e, axis=-1, keepdims=True)
            a = jnp.where(cbi == r0 + s, jnp.where(rbi >= s, colv, 0.0), a)
        rows.append(a)
    a = rows[0] if len(rows) == 1 else jnp.concatenate(rows, axis=0)
    nr = n_row_blocks * rb
    o = (jnp.dot(a.astype(BF16), v, preferred_element_type=F32)
         + jnp.dot((q[:nr] * jnp.exp(b[:nr])).astype(BF16), s_state.astype(BF16), preferred_element_type=F32))
    kh = (k * jnp.exp(bl - b)).astype(BF16)
    blc = jnp.exp(jnp.broadcast_to(bl, (GLA_DK, GLA_DK)).T)
    upd = lax.dot_general(kh, v, (((0,), (0,)), ((), ())), preferred_element_type=F32)
    s_next = jnp.concatenate([blc] * (GLA_DV // GLA_DK), axis=1) * s_state + upd
    return o, s_next


def _gla_out(o, gr, ong):
    return _rms(o, ong) * (gr * jax.nn.sigmoid(gr))


def _gla_prompt_kernel(q_ref, k_ref, v_ref, gr_ref, zs_ref, f2p_ref, fb_ref, ong_ref, s0_ref,
                       og_ref, sout_ref, s_ref):
    i = pl.program_id(1)
    tb = q_ref.shape[0]
    c = GLA_CHUNK

    @pl.when(i == 0)
    def _():
        s_ref[...] = s0_ref[...]

    def chunk(n, carry):
        r = pl.multiple_of(n * c, c)
        q = q_ref[pl.ds(r, c), :].astype(F32) * (GLA_DK ** -0.5)
        k = k_ref[pl.ds(r, c), :].astype(F32)
        o, s_next = _gla_chunk(q, k, v_ref[pl.ds(r, c), :], zs_ref[pl.ds(r, c), :], f2p_ref[...], fb_ref[...],
                               s_ref[...], c // GLA_ROWBLOCK, GLA_ROWBLOCK)
        s_ref[...] = s_next
        og_ref[pl.ds(r, c), :] = _gla_out(o, gr_ref[pl.ds(r, c), :].astype(F32), ong_ref[...]).astype(og_ref.dtype)
        return carry

    lax.fori_loop(0, tb // c, chunk, 0)

    @pl.when(i == pl.num_programs(1) - 1)
    def _():
        sout_ref[...] = s_ref[...]


def _gla_prompt(z1, zs, f2p, fb, ong, s0, tp):
    tb = _pick(tp, (512, 256, 128, 64))
    kq = GLA_QK_W // GLA_DK
    return pl.pallas_call(
        _gla_prompt_kernel,
        out_shape=(jax.ShapeDtypeStruct((tp, GLA_V_W), BF16),
                   jax.ShapeDtypeStruct((GLA_HEADS, GLA_DK, GLA_DV), F32)),
        grid=(GLA_HEADS, tp // tb),
        in_specs=[pl.BlockSpec((tb, GLA_DK), lambda h, i: (i, h)),
                  pl.BlockSpec((tb, GLA_DK), lambda h, i: (i, kq + h)),
                  pl.BlockSpec((tb, GLA_DV), lambda h, i: (i, kq + h)),
                  pl.BlockSpec((tb, GLA_DV), lambda h, i: (i, 2 * kq + h)),
                  pl.BlockSpec((tb, LANES), lambda h, i: (i, 0)),
                  pl.BlockSpec((None, LANES, GLA_DK), lambda h, i: (h, 0, 0)),
                  pl.BlockSpec((None, 1, GLA_DK), lambda h, i: (h, 0, 0)),
                  pl.BlockSpec((1, GLA_DV), lambda h, i: (0, 0)),
                  pl.BlockSpec((None, GLA_DK, GLA_DV), lambda h, i: (h, 0, 0))],
        out_specs=(pl.BlockSpec((tb, GLA_DV), lambda h, i: (i, h)),
                   pl.BlockSpec((None, GLA_DK, GLA_DV), lambda h, i: (h, 0, 0))),
        scratch_shapes=[pltpu.VMEM((GLA_DK, GLA_DV), F32)],
        compiler_params=_cparams(("parallel", "arbitrary")),
        name="gla_prompt",
    )(z1, z1, z1, z1, zs, f2p, fb, ong, s0)


def _gla_sample_kernel(q_ref, k_ref, v_ref, gr_ref, zs_ref, f2p_ref, fb_ref, ong_ref, s0_ref, og_ref, sout_ref):
    t = q_ref.shape[0]
    pad = lambda x: jnp.concatenate([x, jnp.zeros((LANES - t, x.shape[1]), x.dtype)], axis=0)
    q = pad(q_ref[...] * (GLA_DK ** -0.5))
    k = pad(k_ref[...])
    v = pad(v_ref[...]).astype(BF16)
    zs = pad(zs_ref[...])
    rowi = lax.broadcasted_iota(jnp.int32, (LANES, GLA_DK), 0)
    fb = jnp.where(rowi < t, fb_ref[...], 1e4)
    o, s_next = _gla_chunk(q, k, v, zs, f2p_ref[...], fb, s0_ref[...], 1, t)
    og_ref[...] = _gla_out(o, gr_ref[...], ong_ref[...])
    sout_ref[...] = s_next


def _gla_sample(z1s, zss, f2p, fb, ong, s0, n_new):
    ts = z1s.shape[0]
    b = ts // n_new
    kq = GLA_QK_W // GLA_DK
    return pl.pallas_call(
        _gla_sample_kernel,
        out_shape=(jax.ShapeDtypeStruct((ts, GLA_V_W), F32),
                   jax.ShapeDtypeStruct((b, GLA_HEADS, GLA_DK, GLA_DV), F32)),
        grid=(b, GLA_HEADS),
        in_specs=[pl.BlockSpec((n_new, GLA_DK), lambda bi, h: (bi, h)),
                  pl.BlockSpec((n_new, GLA_DK), lambda bi, h: (bi, kq + h)),
                  pl.BlockSpec((n_new, GLA_DV), lambda bi, h: (bi, kq + h)),
                  pl.BlockSpec((n_new, GLA_DV), lambda bi, h: (bi, 2 * kq + h)),
                  pl.BlockSpec((n_new, LANES), lambda bi, h: (bi, 0)),
                  pl.BlockSpec((None, LANES, GLA_DK), lambda bi, h: (h, 0, 0)),
                  pl.BlockSpec((None, 1, GLA_DK), lambda bi, h: (h, 0, 0)),
                  pl.BlockSpec((1, GLA_DV), lambda bi, h: (0, 0)),
                  pl.BlockSpec((None, None, GLA_DK, GLA_DV), lambda bi, h: (bi, h, 0, 0))],
        out_specs=(pl.BlockSpec((n_new, GLA_DV), lambda bi, h: (bi, h)),
                   pl.BlockSpec((None, None, GLA_DK, GLA_DV), lambda bi, h: (bi, h, 0, 0))),
        compiler_params=_cparams(("parallel", "parallel")),
        name="gla_sample",
    )(z1s, z1s, z1s, z1s, zss, f2p, fb, ong, s0)


def _merge_kernel(ogp_ref, ogs_ref, ofp_ref, ofs_ref, gate_ref, wpa_ref, wpb_ref, o_ref, *, n_prompt_blocks):
    i = pl.program_id(0)

    def run(og, of):
        d = o_ref.shape[1]
        ma = gate_ref[:, :d].astype(F32)
        mb = gate_ref[:, d:].astype(F32)
        a = jnp.dot(og, wpa_ref[...], preferred_element_type=F32)
        b = jnp.dot(of, wpb_ref[...], preferred_element_type=F32)
        o_ref[...] = (jax.nn.sigmoid(ma) * a + jax.nn.sigmoid(mb) * b).astype(o_ref.dtype)

    @pl.when(i < n_prompt_blocks)
    def _():
        run(ogp_ref[...], ofp_ref[...])

    @pl.when(i >= n_prompt_blocks)
    def _():
        run(ogs_ref[...], ofs_ref[...])


def _merge(ogp, ogs, ofp, ofs, z3, wpa, wpb):
    tp, ts = ogp.shape[0], ogs.shape[0]
    d = wpa.shape[1]
    tb = ROW_BLOCK
    npb, nb = tp // tb, (tp + ts) // tb
    pidx = lambda i: (jnp.minimum(i, npb - 1), 0)
    sidx = lambda i: (jnp.maximum(i - npb, 0), 0)
    const = lambda i: (0, 0)
    return pl.pallas_call(
        functools.partial(_merge_kernel, n_prompt_blocks=npb),
        out_shape=jax.ShapeDtypeStruct((tp + ts, d), BF16),
        grid=(nb,),
        in_specs=[pl.BlockSpec((tb, GLA_V_W), pidx), pl.BlockSpec((tb, GLA_V_W), sidx),
                  pl.BlockSpec((tb, FOX_W), pidx), pl.BlockSpec((tb, FOX_W), sidx),
                  pl.BlockSpec((tb, 2 * d), lambda i: (i, 0)),
                  pl.BlockSpec(wpa.shape, const), pl.BlockSpec(wpb.shape, const)],
        out_specs=pl.BlockSpec((tb, d), lambda i: (i, 0)),
        compiler_params=_cparams(("parallel",)),
        name="merge",
    )(ogp, ogs, ofp, ofs, z3, wpa, wpb)


def _outproj_kernel(m_ref, w_ref, xp_ref, xs_ref, g1p_ref, g1s_ref, n2g_ref, scp_ref, shp_ref, scs_ref, shs_ref,
                    x1_ref, h2_ref, *, n_prompt_blocks):
    i = pl.program_id(0)
    y = jnp.dot(m_ref[...], w_ref[...], preferred_element_type=F32)

    def run(x, g1, sc, sh):
        x1 = x + g1 * y
        x1_ref[...] = x1
        h2_ref[...] = (_rms(x1, n2g_ref[...]) * (1.0 + sc) + sh).astype(h2_ref.dtype)

    @pl.when(i < n_prompt_blocks)
    def _():
        run(xp_ref[...], g1p_ref[...], scp_ref[...], shp_ref[...])

    @pl.when(i >= n_prompt_blocks)
    def _():
        run(xs_ref[...], g1s_ref[...], scs_ref[...], shs_ref[...])


def _outproj(merged, wout, xp, xs, g1p, g1s, n2g, scp, shp, scs, shs):
    tp, d = xp.shape
    ts = xs.shape[0]
    tb = ROW_BLOCK
    npb, nb = tp // tb, (tp + ts) // tb
    pidx = lambda i: (jnp.minimum(i, npb - 1), 0)
    sidx = lambda i: (jnp.maximum(i - npb, 0), 0)
    const = lambda i: (0, 0)
    row = pl.BlockSpec((1, d), const)
    srow = pl.BlockSpec((tb, d), sidx)
    return pl.pallas_call(
        functools.partial(_outproj_kernel, n_prompt_blocks=npb),
        out_shape=(jax.ShapeDtypeStruct((tp + ts, d), F32), jax.ShapeDtypeStruct((tp + ts, d), BF16)),
        grid=(nb,),
        in_specs=[pl.BlockSpec((tb, d), lambda i: (i, 0)), pl.BlockSpec(wout.shape, const),
                  pl.BlockSpec((tb, d), pidx), srow, row, srow, row, row, row, srow, srow],
        out_specs=(pl.BlockSpec((tb, d), lambda i: (i, 0)), pl.BlockSpec((tb, d), lambda i: (i, 0))),
        compiler_params=_cparams(("parallel",)),
        name="out_proj",
    )(merged, wout, xp, xs, g1p, g1s, n2g, scp, shp, scs, shs)


def _top_rows(x, n):
    out = []
    for _ in range(n):
        m = jnp.max(x, axis=0, keepdims=True)
        out.append(m)
        x = jnp.where(x == m, NEG_INF, x)
    return out


def _peer_topk_kernel(q_ref, sk_ref, th_ref, ka_ref, s2_ref, e2_ref):
    for h in range(PEER_HEADS):
        s = []
        for p in range(2):
            c0 = (2 * h + p) * PEER_NKEYS
            s.append(lax.dot_general(sk_ref[2 * h + p], q_ref[:, c0:c0 + PEER_NKEYS], NT,
                                     preferred_element_type=F32))
        v1 = _top_rows(s[0], PEER_TOPK)
        v2 = _top_rows(s[1], PEER_TOPK)
        cands = [v1[a] + v2[b] for a in range(PEER_TOPK) for b in range(PEER_TOPK)
                 if (a + 1) * (b + 1) <= PEER_TOPK]
        cands += [jnp.full_like(v1[0], NEG_INF)] * (-len(cands) % 8)
        sel = _top_rows(jnp.concatenate(cands, axis=0), PEER_TOPK)
        zsum = sel[0] * 0.0 + 1.0
        for r in range(1, PEER_TOPK):
            zsum = zsum + jnp.exp(sel[r] - sel[0])
        th_ref[h] = sel[PEER_TOPK - 1] - s[0]
        ka_ref[h] = jnp.exp(s[0] - v1[0]) / zsum
        s2_ref[h] = s[1]
        e2_ref[h] = jnp.exp(s[1] - v2[0])


def _peer_topk(q, sk):
    t = q.shape[0]
    tb = ROW_BLOCK
    shp = jax.ShapeDtypeStruct((PEER_HEADS, PEER_NKEYS, t), F32)
    ospec = pl.BlockSpec((PEER_HEADS, PEER_NKEYS, tb), lambda i: (0, 0, i))
    return pl.pallas_call(
        _peer_topk_kernel,
        out_shape=(shp, shp, shp, shp),
        grid=(t // tb,),
        in_specs=[pl.BlockSpec((tb, q.shape[1]), lambda i: (i, 0)),
                  pl.BlockSpec(sk.shape, lambda i: (0, 0, 0))],
        out_specs=(ospec, ospec, ospec, ospec),
        compiler_params=_cparams(("parallel",)),
        name="peer_topk",
    )(q, sk)


def _peer_main_kernel(h2_ref, u_ref, v_ref, th_ref, ka_ref, s2_ref, e2_ref, o_ref, w_even_ref, w_odd_ref):
    j = pl.program_id(1)
    last = pl.num_programs(1) - 1
    te = u_ref.shape[0]

    def gated_activations(w_ref):
        at = lax.dot_general(u_ref[...], h2_ref[...], NT, preferred_element_type=F32)
        for a in range(te // PEER_NKEYS):
            x = at[a * PEER_NKEYS:(a + 1) * PEER_NKEYS]
            ge = 0.5 * x * (1.0 + lax.erf(x * (2.0 ** -0.5)))
            g = jnp.zeros_like(x)
            for h in range(PEER_HEADS):
                g = g + jnp.where(s2_ref[h] >= th_ref[h, a:a + 1, :], e2_ref[h] * ka_ref[h, a:a + 1, :], 0.0)
            w_ref[a * PEER_NKEYS:(a + 1) * PEER_NKEYS, :] = (g * ge).astype(w_ref.dtype)

    def accumulate(w_ref):
        o_ref[...] += lax.dot_general(w_ref[...], v_ref[...], (((0,), (0,)), ((), ())),
                                      preferred_element_type=F32)

    @pl.when(j == 0)
    def _():
        o_ref[...] = jnp.zeros_like(o_ref)
        gated_activations(w_even_ref)

    middle = jnp.logical_and(j > 0, j < last)

    @pl.when(jnp.logical_and(middle, j % 2 == 1))
    def _():
        gated_activations(w_odd_ref)
        accumulate(w_even_ref)

    @pl.when(jnp.logical_and(middle, j % 2 == 0))
    def _():
        gated_activations(w_even_ref)
        accumulate(w_odd_ref)

    @pl.when(jnp.logical_and(j == last, j % 2 == 1))
    def _():
        accumulate(w_even_ref)

    @pl.when(jnp.logical_and(j == last, j % 2 == 0))
    def _():
        accumulate(w_odd_ref)


def _peer_main(h2, u, v, th, ka, s2, e2):
    t, d = h2.shape
    n_exp = u.shape[0]
    tm = _pick(t, (768, 512, 256))
    te = 1024
    nj = n_exp // te
    a_per = te // PEER_NKEYS
    sel_spec = pl.BlockSpec((PEER_HEADS, a_per, tm), lambda i, j: (0, jnp.minimum(j, nj - 1), i))
    tok_spec = pl.BlockSpec((PEER_HEADS, PEER_NKEYS, tm), lambda i, j: (0, 0, i))
    return pl.pallas_call(
        _peer_main_kernel,
        out_shape=jax.ShapeDtypeStruct((t, d), F32),
        grid=(t // tm, nj + 1),
        in_specs=[pl.BlockSpec((tm, d), lambda i, j: (i, 0)),
                  pl.BlockSpec((te, d), lambda i, j: (jnp.minimum(j, nj - 1), 0)),
                  pl.BlockSpec((te, d), lambda i, j: (jnp.maximum(j - 1, 0), 0)),
                  sel_spec, sel_spec, tok_spec, tok_spec],
        out_specs=pl.BlockSpec((tm, d), lambda i, j: (i, 0)),
        scratch_shapes=[pltpu.VMEM((te, tm), BF16), pltpu.VMEM((te, tm), BF16)],
        compiler_params=_cparams(("parallel", "arbitrary")),
        name="peer_main",
    )(h2, u, v, th, ka, s2, e2)


def _final_kernel(x1_ref, y_ref, g2p_ref, g2s_ref, op_ref, os_ref, *, n_prompt_blocks):
    i = pl.program_id(0)

    @pl.when(i < n_prompt_blocks)
    def _():
        op_ref[...] = x1_ref[...] + g2p_ref[...] * y_ref[...]

    @pl.when(i >= n_prompt_blocks)
    def _():
        os_ref[...] = x1_ref[...] + g2s_ref[...] * y_ref[...]


def _final(x1, y, g2p, g2s, tp):
    t, d = x1.shape
    ts = t - tp
    tb = ROW_BLOCK
    npb, nb = tp // tb, t // tb
    pidx = lambda i: (jnp.minimum(i, npb - 1), 0)
    sidx = lambda i: (jnp.maximum(i - npb, 0), 0)
    return pl.pallas_call(
        functools.partial(_final_kernel, n_prompt_blocks=npb),
        out_shape=(jax.ShapeDtypeStruct((tp, d), F32), jax.ShapeDtypeStruct((ts, d), F32)),
        grid=(nb,),
        in_specs=[pl.BlockSpec((tb, d), lambda i: (i, 0)), pl.BlockSpec((tb, d), lambda i: (i, 0)),
                  pl.BlockSpec((1, d), lambda i: (0, 0)), pl.BlockSpec((tb, d), sidx)],
        out_specs=(pl.BlockSpec((tb, d), pidx), pl.BlockSpec((tb, d), sidx)),
        compiler_params=_cparams(("arbitrary",)),
        name="final_residual",
    )(x1, y, g2p, g2s)


def _layer(layer, xp, xs, c_prompt, c_sample, state_gla, cache_k, cache_v, cache_logf, page_table,
           norm1_g, norm2_g, w_ada, b_ada, w_in, gla_f2, gla_fb, gla_on_g, fox_qn_g, fox_kn_g, fox_fb,
           w_pa, w_pb, w_out, peer_wq, peer_subkeys, peer_u, peer_v):
    bp, tpp, d = xp.shape
    assert bp == 1 and d == D_MODEL
    b, n_new, _ = xs.shape
    tp, ts = bp * tpp, b * n_new
    assert tp % ROW_BLOCK == 0 and ts % ROW_BLOCK == 0
    xp2 = xp.reshape(tp, d)
    xs2 = xs.reshape(ts, d)

    n_c = 1 + b
    n_c_pad = -(-n_c // 8) * 8
    c_all = jnp.concatenate([c_prompt, c_sample, jnp.zeros((n_c_pad - n_c, d), F32)], axis=0)
    mod = _ada(c_all, w_ada, b_ada)
    mod_p = [mod[0:1, i * d:(i + 1) * d] for i in range(6)]
    mod_s = [jnp.repeat(mod[1:n_c, i * d:(i + 1) * d], n_new, axis=0) for i in range(6)]
    sh1p, sc1p, g1p, sh2p, sc2p, g2p = mod_p
    sh1s, sc1s, g1s, sh2s, sc2s, g2s = mod_s

    h = _normmod(xp2, xs2, norm1_g.reshape(1, d), sc1p, sh1p, sc1s, sh1s)

    w1 = jnp.concatenate([w_in[:, :_OFF_GF], w_in[:, _OFF_FQ:_OFF_FK]], axis=1).astype(BF16)
    w2 = w_in[:, _OFF_FK:_OFF_FF].astype(BF16)
    w3 = w_in[:, _OFF_MA:_OFF_END].astype(BF16)
    w4 = jnp.concatenate([w_in[:, _OFF_FF:_OFF_MA], w_in[:, _OFF_GF:_OFF_FQ],
                          jnp.zeros((d, LANES - FOX_HEADS - GLA_GATE_RANK), F32)], axis=1).astype(BF16)
    z1 = _matmul(h, w1, BF16, "in_proj_mixers")
    z2 = _matmul(h, w2, F32, "in_proj_fox_kv")
    z3 = _matmul(h, w3, BF16, "in_proj_gates")
    zs = _matmul(h, w4, F32, "in_proj_small")

    fb_pad = jnp.concatenate([fox_fb, jnp.zeros((LANES - FOX_HEADS,), F32)]).reshape(1, LANES)
    (qn, kb, vb, cum, k_p, v_p, lf_p, k_s, v_s, lf_s) = _fox_prep(
        z1, z2, zs, fox_qn_g.reshape(1, FOX_HD), fox_kn_g.reshape(1, FOX_HD), fb_pad, tp, n_new)
    neg_cum_t = (-LOG2E * cum[:tp, :FOX_HEADS]).T.reshape(FOX_HEADS, 1, tp)
    of_p = _fox_prompt(qn, kb, vb, neg_cum_t, tp)

    qs = qn[tp:].reshape(b, n_new, FOX_HEADS, FOX_HD)
    eye = jnp.eye(FOX_HEADS, dtype=BF16)
    qbd = (qs.transpose(0, 2, 3, 1)[:, :, :, None, :] * eye[None, :, None, :, None])
    qbd = qbd.reshape(b, FOX_W, FOX_HEADS * n_new)
    qbd = jnp.concatenate([qbd, jnp.zeros((b, FOX_W, LANES - FOX_HEADS * n_new), BF16)], axis=2)
    lfn_t = lf_s[:, :FOX_HEADS].reshape(b, n_new, FOX_HEADS).transpose(0, 2, 1)
    lfn_t = jnp.concatenate([lfn_t, jnp.zeros((b, FOX_HEADS, LANES - n_new), F32)], axis=2)
    of_s = _fox_sample(layer, page_table, qbd, k_s.reshape(b, n_new, FOX_W), v_s.reshape(b, n_new, FOX_W), lfn_t,
                       cache_k, cache_v, cache_logf).reshape(ts, FOX_W).astype(BF16)

    f2p = jnp.zeros((LANES, GLA_QK_W), F32).at[FOX_HEADS:FOX_HEADS + GLA_GATE_RANK].set(gla_f2)
    f2p = f2p.reshape(LANES, GLA_HEADS, GLA_DK).transpose(1, 0, 2).astype(BF16)
    fbh = gla_fb.reshape(GLA_HEADS, 1, GLA_DK)
    ong = gla_on_g.reshape(1, GLA_DV)
    og_p, sg_p = _gla_prompt(z1, zs, f2p, fbh, ong, jnp.zeros((GLA_HEADS, GLA_DK, GLA_DV), F32), tp)
    og_s, sg_s = _gla_sample(z1[tp:].astype(F32), zs[tp:], f2p, fbh, ong, state_gla, n_new)

    merged = _merge(og_p, og_s.astype(BF16), of_p, of_s, z3, w_pa.astype(BF16), w_pb.astype(BF16))
    x1, h2 = _outproj(merged, w_out.astype(BF16), xp2, xs2, g1p, g1s, norm2_g.reshape(1, d),
                      sc2p, sh2p, sc2s, sh2s)

    q = _matmul(h2, peer_wq.astype(BF16), BF16, "peer_query")
    sk = peer_subkeys.reshape(PEER_HEADS * 2, PEER_NKEYS, -1).astype(BF16)
    th, ka, s2, e2 = _peer_topk(q, sk)
    y = _peer_main(h2, peer_u.astype(BF16), peer_v.astype(BF16), th, ka, s2, e2)
    out_p, out_s = _final(x1, y, g2p, g2s, tp)

    return (out_p.reshape(bp, tpp, d), out_s.reshape(b, n_new, d),
            sg_p.reshape(bp, GLA_HEADS, GLA_DK, GLA_DV), sg_s,
            k_p.reshape(bp, tpp, FOX_HEADS, FOX_HD), v_p.reshape(bp, tpp, FOX_HEADS, FOX_HD),
            lf_p[:, :FOX_HEADS].reshape(bp, tpp, FOX_HEADS),
            k_s.reshape(b, n_new, FOX_HEADS, FOX_HD), v_s.reshape(b, n_new, FOX_HEADS, FOX_HD),
            lf_s[:, :FOX_HEADS].reshape(b, n_new, FOX_HEADS))


def kernel(x_prompt, x_sample, c_prompt, c_sample, state_gla, cache_k, cache_v, cache_logf, page_table,
           norm1_g, norm2_g, w_ada, b_ada, w_in, gla_f2, gla_fb, gla_on_g, fox_qn_g, fox_kn_g, fox_fb,
           w_pa, w_pb, w_out, peer_wq, peer_subkeys, peer_u, peer_v):
    depth = w_in.shape[0]
    xp, xs = x_prompt, x_sample
    outs = []
    for l in range(depth):
        res = _layer(l, xp, xs, c_prompt, c_sample, state_gla[l], cache_k, cache_v, cache_logf, page_table,
                     norm1_g[l], norm2_g[l], w_ada[l], b_ada[l], w_in[l], gla_f2[l], gla_fb[l], gla_on_g[l],
                     fox_qn_g[l], fox_kn_g[l], fox_fb[l], w_pa[l], w_pb[l], w_out[l],
                     peer_wq[l], peer_subkeys[l], peer_u[l], peer_v[l])
        xp, xs = res[0], res[1]
        outs.append(res[2:])
    stacked = [jnp.stack([o[i] for o in outs]) for i in range(8)]
    return (xp, xs, *stacked)
```

```python
import functools

import jax
import jax.numpy as jnp
from jax import lax
from jax.experimental import pallas as pl
from jax.experimental.pallas import tpu as pltpu

F32 = jnp.float32
BF16 = jnp.bfloat16

D_MODEL = 2048
GLA_HEADS = 4
GLA_DK = 128
GLA_DV = 256
GLA_GATE_RANK = 16
GLA_GATE_TAU = 16.0
GLA_CHUNK = 64
GLA_ROWBLOCK = 16
FOX_HEADS = 8
FOX_HD = 128
FOX_W = FOX_HEADS * FOX_HD
PEER_HEADS = 8
PEER_NKEYS = 128
PEER_TOPK = 16
PAGE_SIZE = 128
EPS = 1e-6

GLA_QK_W = GLA_HEADS * GLA_DK
GLA_V_W = GLA_HEADS * GLA_DV
_OFF_GF = 2 * GLA_QK_W + 2 * GLA_V_W
_OFF_FQ = _OFF_GF + GLA_GATE_RANK
_OFF_FK = _OFF_FQ + FOX_W
_OFF_FV = _OFF_FK + FOX_W
_OFF_FF = _OFF_FV + FOX_W
_OFF_MA = _OFF_FF + FOX_HEADS
_OFF_MB = _OFF_MA + D_MODEL
_OFF_END = _OFF_MB + D_MODEL

LANES = 128
ROW_BLOCK = 256
VMEM_LIMIT = 56 * 1024 * 1024

NT = (((1,), (1,)), ((), ()))
NEG_INF = float("-inf")
LOG2E = 1.4426950408889634


def _cparams(sem):
    return pltpu.CompilerParams(dimension_semantics=sem, vmem_limit_bytes=VMEM_LIMIT)


def _split3(x):
    hi = x.astype(BF16)
    r = x - hi.astype(F32)
    mid = r.astype(BF16)
    lo = (r - mid.astype(F32)).astype(BF16)
    return hi, mid, lo


def _dot01(m01, x):
    acc = None
    for p in _split3(x):
        t = jnp.dot(m01, p, preferred_element_type=F32)
        acc = t if acc is None else acc + t
    return acc


def _dotx01(x, m01):
    acc = None
    for p in _split3(x):
        t = jnp.dot(p, m01, preferred_element_type=F32)
        acc = t if acc is None else acc + t
    return acc


def _log_sigmoid(z):
    return jnp.minimum(z, 0.0) - jnp.log1p(jnp.exp(-jnp.abs(z)))


def _rms(x, g):
    return x * lax.rsqrt(jnp.mean(x * x, axis=-1, keepdims=True) + EPS) * g


def _ada_kernel(c_ref, w_ref, b_ref, o_ref):
    o_ref[...] = jnp.dot(c_ref[...].astype(BF16), w_ref[...].astype(BF16),
                         preferred_element_type=F32) + b_ref[...]


def _ada(c_all, w_ada, b_ada):
    m, k = c_all.shape
    n = w_ada.shape[1]
    tn = 1024
    return pl.pallas_call(
        _ada_kernel,
        out_shape=jax.ShapeDtypeStruct((m, n), F32),
        grid=(n // tn,),
        in_specs=[pl.BlockSpec((m, k), lambda j: (0, 0)),
                  pl.BlockSpec((k, tn), lambda j: (0, j)),
                  pl.BlockSpec((1, tn), lambda j: (0, j))],
        out_specs=pl.BlockSpec((m, tn), lambda j: (0, j)),
        compiler_params=_cparams(("parallel",)),
        name="ada_mod",
    )(c_all, w_ada, b_ada.reshape(1, n))


def _normmod_kernel(xp_ref, xs_ref, g_ref, scp_ref, shp_ref, scs_ref, shs_ref, o_ref, *, n_prompt_blocks):
    i = pl.program_id(0)

    @pl.when(i < n_prompt_blocks)
    def _():
        h = _rms(xp_ref[...], g_ref[...]) * (1.0 + scp_ref[...]) + shp_ref[...]
        o_ref[...] = h.astype(o_ref.dtype)

    @pl.when(i >= n_prompt_blocks)
    def _():
        h = _rms(xs_ref[...], g_ref[...]) * (1.0 + scs_ref[...]) + shs_ref[...]
        o_ref[...] = h.astype(o_ref.dtype)


def _normmod(xp, xs, g, scp, shp, scs, shs):
    tp, d = xp.shape
    ts = xs.shape[0]
    tb = ROW_BLOCK
    npb, nsb = tp // tb, ts // tb
    pidx = lambda i: (jnp.minimum(i, npb - 1), 0)
    sidx = lambda i: (jnp.maximum(i - npb, 0), 0)
    row = pl.BlockSpec((1, d), lambda i: (0, 0))
    return pl.pallas_call(
        functools.partial(_normmod_kernel, n_prompt_blocks=npb),
        out_shape=jax.ShapeDtypeStruct((tp + ts, d), BF16),
        grid=(npb + nsb,),
        in_specs=[pl.BlockSpec((tb, d), pidx), pl.BlockSpec((tb, d), sidx), row, row, row,
                  pl.BlockSpec((tb, d), sidx), pl.BlockSpec((tb, d), sidx)],
        out_specs=pl.BlockSpec((tb, d), lambda i: (i, 0)),
        compiler_params=_cparams(("parallel",)),
        name="norm_mod",
    )(xp, xs, g, scp, shp, scs, shs)


def _mm_kernel(a_ref, w_ref, o_ref):
    o_ref[...] = jnp.dot(a_ref[...], w_ref[...], preferred_element_type=F32).astype(o_ref.dtype)


def _pick(n, cands):
    for c in cands:
        if n % c == 0:
            return c
    return n


def _matmul(a, w, out_dtype, name):
    m, k = a.shape
    n = w.shape[1]
    tm = _pick(m, (1024, 768, 512, 256))
    tn = _pick(n, (1024, 512, 256, 128))
    return pl.pallas_call(
        _mm_kernel,
        out_shape=jax.ShapeDtypeStruct((m, n), out_dtype),
        grid=(m // tm, n // tn),
        in_specs=[pl.BlockSpec((tm, k), lambda i, j: (i, 0)),
                  pl.BlockSpec((k, tn), lambda i, j: (0, j))],
        out_specs=pl.BlockSpec((tm, tn), lambda i, j: (i, j)),
        compiler_params=_cparams(("parallel", "parallel")),
        name=name,
    )(a, w)


def _fox_prep_kernel(fq_ref, fkv_ref, zs_ref, qg_ref, kg_ref, fb_ref,
                     qn_ref, kb_ref, vb_ref, cum_ref, kp_ref, vp_ref, lfp_ref, ks_ref, vs_ref, lfs_ref,
                     carry_ref, *, n_prompt_blocks, seg_len):
    i = pl.program_id(0)
    tb = fq_ref.shape[0]
    is_prompt = i < n_prompt_blocks

    @pl.when(i == 0)
    def _():
        carry_ref[...] = jnp.zeros_like(carry_ref)

    scale = FOX_HD ** -0.5 * jnp.where(is_prompt, LOG2E, 1.0).astype(F32)
    fq = fq_ref[...].astype(F32)
    fkv = fkv_ref[...]
    kn = []
    for h in range(FOX_HEADS):
        sl = slice(h * FOX_HD, (h + 1) * FOX_HD)
        qn_ref[:, sl] = (_rms(fq[:, sl], qg_ref[...]) * scale).astype(qn_ref.dtype)
        kn.append(_rms(fkv[:, sl], kg_ref[...]))
    kn = jnp.concatenate(kn, axis=1)
    v = fkv[:, FOX_W:]
    kb_ref[...] = kn.astype(kb_ref.dtype)
    vb_ref[...] = v.astype(vb_ref.dtype)

    lf = _log_sigmoid(zs_ref[...] + fb_ref[...])
    row = lax.broadcasted_iota(jnp.int32, (tb, tb), 0)
    col = lax.broadcasted_iota(jnp.int32, (tb, tb), 1)

    @pl.when(is_prompt)
    def _():
        tri = jnp.where(col <= row, 1.0, 0.0).astype(BF16)
        cum = _dot01(tri, lf) + carry_ref[...]
        carry_ref[...] = cum[tb - 1:tb, :]
        cum_ref[...] = cum
        kp_ref[...] = kn
        vp_ref[...] = v
        lfp_ref[...] = lf

    @pl.when(jnp.logical_not(is_prompt))
    def _():
        tri = jnp.where(col <= row, jnp.where((row // seg_len) == (col // seg_len), 1.0, 0.0), 0.0).astype(BF16)
        cum_ref[...] = _dot01(tri, lf)
        ks_ref[...] = kn
        vs_ref[...] = v
        lfs_ref[...] = lf


def _fox_prep(z1, z2, zs, qg, kg, fb_pad, tp, seg_len):
    t_all = z1.shape[0]
    ts = t_all - tp
    tb = ROW_BLOCK
    npb, nb = tp // tb, t_all // tb
    pidx = lambda i: (jnp.minimum(i, npb - 1), 0)
    sidx = lambda i: (jnp.maximum(i - npb, 0), 0)
    allidx = lambda i: (i, 0)
    row = pl.BlockSpec((1, LANES), lambda i: (0, 0))
    fq_block = (_OFF_GF) // FOX_W
    return pl.pallas_call(
        functools.partial(_fox_prep_kernel, n_prompt_blocks=npb, seg_len=seg_len),
        out_shape=(jax.ShapeDtypeStruct((t_all, FOX_W), BF16),
                   jax.ShapeDtypeStruct((t_all, FOX_W), BF16),
                   jax.ShapeDtypeStruct((t_all, FOX_W), BF16),
                   jax.ShapeDtypeStruct((t_all, LANES), F32),
                   jax.ShapeDtypeStruct((tp, FOX_W), F32),
                   jax.ShapeDtypeStruct((tp, FOX_W), F32),
                   jax.ShapeDtypeStruct((tp, LANES), F32),
                   jax.ShapeDtypeStruct((ts, FOX_W), F32),
                   jax.ShapeDtypeStruct((ts, FOX_W), F32),
                   jax.ShapeDtypeStruct((ts, LANES), F32)),
        grid=(nb,),
        in_specs=[pl.BlockSpec((tb, FOX_W), lambda i: (i, fq_block)),
                  pl.BlockSpec((tb, 2 * FOX_W), allidx),
                  pl.BlockSpec((tb, LANES), allidx), row, row, row],
        out_specs=(pl.BlockSpec((tb, FOX_W), allidx), pl.BlockSpec((tb, FOX_W), allidx),
                   pl.BlockSpec((tb, FOX_W), allidx), pl.BlockSpec((tb, LANES), allidx),
                   pl.BlockSpec((tb, FOX_W), pidx), pl.BlockSpec((tb, FOX_W), pidx),
                   pl.BlockSpec((tb, LANES), pidx),
                   pl.BlockSpec((tb, FOX_W), sidx), pl.BlockSpec((tb, FOX_W), sidx),
                   pl.BlockSpec((tb, LANES), sidx)),
        scratch_shapes=[pltpu.VMEM((1, LANES), F32)],
        compiler_params=_cparams(("arbitrary",)),
        name="fox_prep",
    )(z1, z2, zs, qg, kg, fb_pad)


FOX_HEADS_PER_STEP = 4


def _fox_prompt_kernel(q_ref, k_ref, v_ref, nc_ref, o_ref, *, tk):
    qi = pl.program_id(1)
    tq = q_ref.shape[0]
    nh = q_ref.shape[1] // FOX_HD
    n_full = (qi * tq) // tk
    n_diag = -(-tq // tk)

    def block(c0, masked, carry):
        out = []
        for h in range(nh):
            m, l, acc = carry[h]
            sl = slice(h * FOX_HD, (h + 1) * FOX_HD)
            s = lax.dot_general(q_ref[:, sl], k_ref[pl.ds(c0, tk), sl], NT, preferred_element_type=F32)
            s = s + nc_ref[h, :, pl.ds(c0, tk)]
            if masked:
                row = qi * tq + lax.broadcasted_iota(jnp.int32, (tq, tk), 0)
                col = c0 + lax.broadcasted_iota(jnp.int32, (tq, tk), 1)
                s = jnp.where(col <= row, s, NEG_INF)
            m_new = jnp.maximum(m, jnp.max(s, axis=1, keepdims=True))
            alpha = jnp.exp2(m - m_new)
            p = jnp.exp2(s - m_new)
            l = alpha * l + jnp.sum(p, axis=1, keepdims=True)
            acc = alpha * acc + jnp.dot(p.astype(BF16), v_ref[pl.ds(c0, tk), sl], preferred_element_type=F32)
            out.append((m_new, l, acc))
        return tuple(out)

    init = tuple((jnp.full((tq, 1), NEG_INF, F32), jnp.zeros((tq, 1), F32), jnp.zeros((tq, FOX_HD), F32))
                 for _ in range(nh))
    carry = lax.fori_loop(0, n_full, lambda i, c: block(pl.multiple_of(i * tk, tk), False, c), init)
    final = lax.fori_loop(n_full, n_full + n_diag, lambda i, c: block(pl.multiple_of(i * tk, tk), True, c), carry)
    for h in range(nh):
        _, l, acc = final[h]
        o_ref[:, h * FOX_HD:(h + 1) * FOX_HD] = (acc / l).astype(o_ref.dtype)


def _fox_prompt(qn, kb, vb, neg_cum_t, tp):
    tk = _pick(tp, (512, 256, 128))
    tq = max(tk // 2, 128)
    nh = FOX_HEADS_PER_STEP
    w = nh * FOX_HD
    return pl.pallas_call(
        functools.partial(_fox_prompt_kernel, tk=tk),
        out_shape=jax.ShapeDtypeStruct((tp, FOX_W), BF16),
        grid=(FOX_HEADS // nh, tp // tq),
        in_specs=[pl.BlockSpec((tq, w), lambda g, i: (i, g)),
                  pl.BlockSpec((tp, w), lambda g, i: (0, g)),
                  pl.BlockSpec((tp, w), lambda g, i: (0, g)),
                  pl.BlockSpec((nh, 1, tp), lambda g, i: (g, 0, 0))],
        out_specs=pl.BlockSpec((tq, w), lambda g, i: (i, g)),
        compiler_params=_cparams(("parallel", "parallel")),
        name="fox_prompt",
    )(qn, kb, vb, neg_cum_t)


def _expand_heads(x_t, n_q):
    hi, mid, lo = _split3(x_t)
    n = x_t.shape[0]
    stacked = jnp.concatenate([hi.astype(F32), mid.astype(F32), lo.astype(F32),
                               jnp.zeros((LANES - 3 * n, LANES), F32)], axis=0)
    k = lax.broadcasted_iota(jnp.int32, (LANES, LANES), 0)
    c = lax.broadcasted_iota(jnp.int32, (LANES, LANES), 1)
    e3 = jnp.where(k < 3 * n, jnp.where(c < n * n_q, jnp.where((k % n) == (c // n_q), 1.0, 0.0), 0.0), 0.0)
    return jnp.dot(stacked.T.astype(BF16), e3.astype(BF16), preferred_element_type=F32)


def _page_to_rows(page_ref, buf_ref, p):
    for h in range(FOX_HEADS):
        x = page_ref[pl.ds(h, PAGE_SIZE, stride=FOX_HEADS), :]
        buf_ref[p * PAGE_SIZE:(p + 1) * PAGE_SIZE, h * FOX_HD:(h + 1) * FOX_HD] = x.astype(BF16)


def _fox_sample_kernel(pt_ref, qbd_ref, kn_ref, vn_ref, lfn_ref, *rest, pages_per_step, n_groups):
    pg = pages_per_step
    k_refs = rest[:pg]
    lf_refs = rest[pg:2 * pg]
    v_refs = rest[2 * pg:3 * pg]
    o_ref = rest[3 * pg]
    s_all, s_new, m_ref, l_ref, acc_ref, carry_ref, kv_buf, xin_ref, eall_ref = rest[3 * pg + 1:]
    j = pl.program_id(1)
    n_new = kn_ref.shape[0]
    hp = LANES // pg
    rows = pg * PAGE_SIZE
    ki = lax.broadcasted_iota(jnp.int32, (LANES, LANES), 0)
    ci = lax.broadcasted_iota(jnp.int32, (LANES, LANES), 1)

    @pl.when(j == 0)
    def _():
        pad = jnp.zeros((LANES - n_new, FOX_W), F32)
        knp = jnp.concatenate([kn_ref[...], pad], axis=0).astype(BF16)
        incl = jnp.where(ki <= ci, 1.0, 0.0).astype(BF16)
        cn_t = _dotx01(lfn_ref[...], incl)
        s = jnp.dot(knp, qbd_ref[...], preferred_element_type=F32) - _expand_heads(cn_t, n_new)
        s = jnp.where(ki <= ci % n_new, s, NEG_INF)
        s = jnp.where(ki < n_new, s, NEG_INF)
        s = jnp.where(ci < FOX_HEADS * n_new, s, NEG_INF)
        s_new[...] = s
        m_ref[...] = jnp.max(s, axis=0, keepdims=True)
        carry_ref[...] = jnp.zeros_like(carry_ref)
        xin_ref[...] = jnp.zeros_like(xin_ref)
        kk = lax.broadcasted_iota(jnp.int32, eall_ref.shape, 0)
        cc = lax.broadcasted_iota(jnp.int32, eall_ref.shape, 1)
        hit = kk == (cc // LANES) * hp + (cc % LANES) // n_new
        eall_ref[...] = jnp.where(hit, jnp.where(cc % LANES < FOX_HEADS * n_new, 1.0, 0.0), 0.0).astype(BF16)

    @pl.when(j < n_groups)
    def _():
        g = n_groups - 1 - j
        for p in range(pg):
            _page_to_rows(k_refs[p], kv_buf, p)
            xin_ref[:, p * hp:p * hp + FOX_HEADS] = lf_refs[p][...]
        s = jnp.dot(kv_buf[...], qbd_ref[...], preferred_element_type=F32)
        x = xin_ref[...]
        later_t = jnp.where(ci > ki, 1.0, 0.0).astype(BF16)
        same_head = (ki % hp) == (ci % hp)
        later_page = jnp.where(same_head, jnp.where(ki // hp > ci // hp, 1.0, 0.0), 0.0).astype(BF16)
        any_page = jnp.where(same_head, 1.0, 0.0).astype(BF16)
        tot = jnp.broadcast_to(jnp.sum(x, axis=0, keepdims=True), (8, LANES))
        tail = _dot01(later_t, x) + (_dotx01(tot, later_page)[0:1] + carry_ref[...])
        carry_ref[...] = carry_ref[...] + _dotx01(tot, any_page)[0:1]
        ex = jnp.dot(jnp.concatenate(list(_split3(tail)), axis=0), eall_ref[...], preferred_element_type=F32)
        blocks = []
        for p in range(pg):
            cols = slice(p * LANES, (p + 1) * LANES)
            add = ex[0:LANES, cols] + ex[LANES:2 * LANES, cols] + ex[2 * LANES:3 * LANES, cols]
            blocks.append(s[p * PAGE_SIZE:(p + 1) * PAGE_SIZE] + add)
        s = jnp.concatenate(blocks, axis=0)
        s_all[pl.ds(pl.multiple_of(g * rows, rows), rows), :] = s
        m_ref[...] = jnp.maximum(m_ref[...], jnp.max(s, axis=0, keepdims=True))

    @pl.when(j == n_groups)
    def _():
        pad = jnp.zeros((LANES - n_new, FOX_W), F32)
        vnp = jnp.concatenate([vn_ref[...], pad], axis=0).astype(BF16)
        p = jnp.exp(s_new[...] - m_ref[...])
        l_ref[...] = jnp.sum(p, axis=0, keepdims=True)
        acc_ref[...] = jnp.dot(p.T.astype(BF16), vnp, preferred_element_type=F32)

    @pl.when(j >= n_groups)
    def _():
        g = 2 * n_groups - 1 - j
        for p in range(pg):
            _page_to_rows(v_refs[p], kv_buf, p)
        pexp = jnp.exp(s_all[pl.ds(pl.multiple_of(g * rows, rows), rows), :] - m_ref[...])
        l_ref[...] += jnp.sum(pexp, axis=0, keepdims=True)
        acc_ref[...] += jnp.dot(pexp.T.astype(BF16), kv_buf[...], preferred_element_type=F32)

    @pl.when(j == 2 * n_groups - 1)
    def _():
        l_col = jnp.broadcast_to(l_ref[...], (LANES, LANES)).T
        for h in range(FOX_HEADS):
            rows = slice(h * n_new, (h + 1) * n_new)
            cols = slice(h * FOX_HD, (h + 1) * FOX_HD)
            o_ref[:, cols] = acc_ref[rows, cols] / l_col[rows, :]


def _fox_sample(layer, page_table, qbd, kn, vn, lfn_t, ck, cv, clf):
    ck = ck.reshape(ck.shape[0], ck.shape[1], PAGE_SIZE * FOX_HEADS, FOX_HD)
    cv = cv.reshape(cv.shape[0], cv.shape[1], PAGE_SIZE * FOX_HEADS, FOX_HD)
    b, n_pages = page_table.shape
    n_new = kn.shape[1]
    pg = _pick(n_pages, (16, 8, 4, 2))
    ng = n_pages // pg
    assert FOX_HEADS * pg <= LANES and FOX_HEADS * n_new <= LANES

    def page_map(r, phase_v, n_trailing):
        def index(bi, j, pt):
            step = jnp.maximum(j - ng, 0) if phase_v else jnp.minimum(j, ng - 1)
            return (layer, pt[bi, (ng - 1 - step) * pg + r]) + (0,) * n_trailing
        return index

    per_b = lambda bi, j, pt: (bi, 0, 0)
    page_blk = (None, None, PAGE_SIZE * FOX_HEADS, FOX_HD)
    in_specs = [pl.BlockSpec((None, FOX_W, LANES), per_b),
                pl.BlockSpec((None, n_new, FOX_W), per_b),
                pl.BlockSpec((None, n_new, FOX_W), per_b),
                pl.BlockSpec((None, FOX_HEADS, LANES), per_b)]
    in_specs += [pl.BlockSpec(page_blk, page_map(r, False, 2)) for r in range(pg)]
    in_specs += [pl.BlockSpec((None, None, PAGE_SIZE, FOX_HEADS), page_map(r, False, 2)) for r in range(pg)]
    in_specs += [pl.BlockSpec(page_blk, page_map(r, True, 2)) for r in range(pg)]
    grid_spec = pltpu.PrefetchScalarGridSpec(
        num_scalar_prefetch=1,
        grid=(b, 2 * ng),
        in_specs=in_specs,
        out_specs=pl.BlockSpec((None, n_new, FOX_W), per_b),
        scratch_shapes=[pltpu.VMEM((n_pages * PAGE_SIZE, LANES), F32),
                        pltpu.VMEM((LANES, LANES), F32),
                        pltpu.VMEM((1, LANES), F32),
                        pltpu.VMEM((1, LANES), F32),
                        pltpu.VMEM((LANES, FOX_W), F32),
                        pltpu.VMEM((1, LANES), F32),
                        pltpu.VMEM((pg * PAGE_SIZE, FOX_W), BF16),
                        pltpu.VMEM((PAGE_SIZE, LANES), F32),
                        pltpu.VMEM((LANES, pg * LANES), BF16)],
    )
    return pl.pallas_call(
        functools.partial(_fox_sample_kernel, pages_per_step=pg, n_groups=ng),
        out_shape=jax.ShapeDtypeStruct((b, n_new, FOX_W), F32),
        grid_spec=grid_spec,
        compiler_params=_cparams(("parallel", "arbitrary")),
        name="fox_sample",
    )(page_table, qbd, kn, vn, lfn_t, *([ck] * pg), *([clf] * pg), *([cv] * pg))


def _gla_chunk(q, k, v, zs, f2p, fb, s_state, n_row_blocks, rb):
    c = q.shape[0]
    z = jnp.dot(zs.astype(BF16), f2p, preferred_element_type=F32) + fb
    la = _log_sigmoid(z) * (1.0 / GLA_GATE_TAU)
    ri = lax.broadcasted_iota(jnp.int32, (c, c), 0)
    ci = lax.broadcasted_iota(jnp.int32, (c, c), 1)
    b = _dot01(jnp.where(ci <= ri, 1.0, 0.0).astype(BF16), la)
    bl = b[c - 1:c]
    rbi = lax.broadcasted_iota(jnp.int32, (rb, c), 0)
    cbi = lax.broadcasted_iota(jnp.int32, (rb, c), 1)
    rows = []
    for i in range(n_row_blocks):
        r0 = i * rb
        qb, kb_, bb = q[r0:r0 + rb], k[r0:r0 + rb], b[r0:r0 + rb]
        a = jnp.zeros((rb, c), F32)
        if i > 0:
            bref = b[r0:r0 + 1]
            qt = qb * jnp.exp(bb - bref)
            kt = k * jnp.exp(jnp.minimum(bref - b, 0.0))
            a = lax.dot_general(qt.astype(BF16), kt.astype(BF16), NT, preferred_element_type=F32)
            a = jnp.where(cbi < r0, a, 0.0)
        for s in range(rb):
            e = jnp.exp(jnp.minimum(bb - bb[s:s + 1], 0.0))
            colv = jnp.sum(qb * kb_[s:s + 1] * e, axis=-1, keepdims=True)
            a = jnp.where(cbi == r0 + s, jnp.where(rbi >= s, colv, 0.0), a)
        rows.append(a)
    a = rows[0] if len(rows) == 1 else jnp.concatenate(rows, axis=0)
    nr = n_row_blocks * rb
    o = (jnp.dot(a.astype(BF16), v, preferred_element_type=F32)
         + jnp.dot((q[:nr] * jnp.exp(b[:nr])).astype(BF16), s_state.astype(BF16), preferred_element_type=F32))
    kh = (k * jnp.exp(bl - b)).astype(BF16)
    blc = jnp.exp(jnp.broadcast_to(bl, (GLA_DK, GLA_DK)).T)
    upd = lax.dot_general(kh, v, (((0,), (0,)), ((), ())), preferred_element_type=F32)
    s_next = jnp.concatenate([blc] * (GLA_DV // GLA_DK), axis=1) * s_state + upd
    return o, s_next


def _gla_out(o, gr, ong):
    return _rms(o, ong) * (gr * jax.nn.sigmoid(gr))


def _gla_prompt_kernel(q_ref, k_ref, v_ref, gr_ref, zs_ref, f2p_ref, fb_ref, ong_ref, s0_ref,
                       og_ref, sout_ref, s_ref):
    i = pl.program_id(0)
    tb = q_ref.shape[0]
    c = GLA_CHUNK

    @pl.when(i == 0)
    def _():
        s_ref[...] = s0_ref[...]

    def chunk(n, carry):
        r = pl.multiple_of(n * c, c)
        zs = zs_ref[pl.ds(r, c), :]
        for h in range(GLA_HEADS):
            qk = slice(h * GLA_DK, (h + 1) * GLA_DK)
            vv = slice(h * GLA_DV, (h + 1) * GLA_DV)
            q = q_ref[pl.ds(r, c), qk].astype(F32) * (GLA_DK ** -0.5)
            k = k_ref[pl.ds(r, c), qk].astype(F32)
            o, s_next = _gla_chunk(q, k, v_ref[pl.ds(r, c), vv], zs, f2p_ref[h], fb_ref[h],
                                   s_ref[h], c // GLA_ROWBLOCK, GLA_ROWBLOCK)
            s_ref[h] = s_next
            og_ref[pl.ds(r, c), vv] = _gla_out(o, gr_ref[pl.ds(r, c), vv].astype(F32),
                                               ong_ref[...]).astype(og_ref.dtype)
        return carry

    lax.fori_loop(0, tb // c, chunk, 0)

    @pl.when(i == pl.num_programs(0) - 1)
    def _():
        sout_ref[...] = s_ref[...]


def _gla_prompt(z1, zs, f2p, fb, ong, s0, tp):
    tb = _pick(tp, (512, 256, 128, 64))
    whole = lambda i: (0, 0, 0)
    return pl.pallas_call(
        _gla_prompt_kernel,
        out_shape=(jax.ShapeDtypeStruct((tp, GLA_V_W), BF16),
                   jax.ShapeDtypeStruct((GLA_HEADS, GLA_DK, GLA_DV), F32)),
        grid=(tp // tb,),
        in_specs=[pl.BlockSpec((tb, GLA_QK_W), lambda i: (i, 0)),
                  pl.BlockSpec((tb, GLA_QK_W), lambda i: (i, 1)),
                  pl.BlockSpec((tb, GLA_V_W), lambda i: (i, 1)),
                  pl.BlockSpec((tb, GLA_V_W), lambda i: (i, 2)),
                  pl.BlockSpec((tb, LANES), lambda i: (i, 0)),
                  pl.BlockSpec((GLA_HEADS, LANES, GLA_DK), whole),
                  pl.BlockSpec((GLA_HEADS, 1, GLA_DK), whole),
                  pl.BlockSpec((1, GLA_DV), lambda i: (0, 0)),
                  pl.BlockSpec((GLA_HEADS, GLA_DK, GLA_DV), whole)],
        out_specs=(pl.BlockSpec((tb, GLA_V_W), lambda i: (i, 0)),
                   pl.BlockSpec((GLA_HEADS, GLA_DK, GLA_DV), whole)),
        scratch_shapes=[pltpu.VMEM((GLA_HEADS, GLA_DK, GLA_DV), F32)],
        compiler_params=_cparams(("arbitrary",)),
        name="gla_prompt",
    )(z1, z1, z1, z1, zs, f2p, fb, ong, s0)


def _gla_sample_kernel(q_ref, k_ref, v_ref, gr_ref, zs_ref, f2p_ref, fb_ref, ong_ref, s0_ref, og_ref, sout_ref):
    t = q_ref.shape[0]
    pad = lambda x: jnp.concatenate([x, jnp.zeros((LANES - t, x.shape[1]), x.dtype)], axis=0)
    q = pad(q_ref[...] * (GLA_DK ** -0.5))
    k = pad(k_ref[...])
    v = pad(v_ref[...]).astype(BF16)
    zs = pad(zs_ref[...])
    rowi = lax.broadcasted_iota(jnp.int32, (LANES, GLA_DK), 0)
    fb = jnp.where(rowi < t, fb_ref[...], 1e4)
    o, s_next = _gla_chunk(q, k, v, zs, f2p_ref[...], fb, s0_ref[...], 1, t)
    og_ref[...] = _gla_out(o, gr_ref[...], ong_ref[...])
    sout_ref[...] = s_next


def _gla_sample(z1s, zss, f2p, fb, ong, s0, n_new):
    ts = z1s.shape[0]
    b = ts // n_new
    kq = GLA_QK_W // GLA_DK
    return pl.pallas_call(
        _gla_sample_kernel,
        out_shape=(jax.ShapeDtypeStruct((ts, GLA_V_W), F32),
                   jax.ShapeDtypeStruct((b, GLA_HEADS, GLA_DK, GLA_DV), F32)),
        grid=(b, GLA_HEADS),
        in_specs=[pl.BlockSpec((n_new, GLA_DK), lambda bi, h: (bi, h)),
                  pl.BlockSpec((n_new, GLA_DK), lambda bi, h: (bi, kq + h)),
                  pl.BlockSpec((n_new, GLA_DV), lambda bi, h: (bi, kq + h)),
                  pl.BlockSpec((n_new, GLA_DV), lambda bi, h: (bi, 2 * kq + h)),
                  pl.BlockSpec((n_new, LANES), lambda bi, h: (bi, 0)),
                  pl.BlockSpec((None, LANES, GLA_DK), lambda bi, h: (h, 0, 0)),
                  pl.BlockSpec((None, 1, GLA_DK), lambda bi, h: (h, 0, 0)),
                  pl.BlockSpec((1, GLA_DV), lambda bi, h: (0, 0)),
                  pl.BlockSpec((None, None, GLA_DK, GLA_DV), lambda bi, h: (bi, h, 0, 0))],
        out_specs=(pl.BlockSpec((n_new, GLA_DV), lambda bi, h: (bi, h)),
                   pl.BlockSpec((None, None, GLA_DK, GLA_DV), lambda bi, h: (bi, h, 0, 0))),
        compiler_params=_cparams(("parallel", "parallel")),
        name="gla_sample",
    )(z1s, z1s, z1s, z1s, zss, f2p, fb, ong, s0)


def _merge_kernel(ogp_ref, ogs_ref, ofp_ref, ofs_ref, gate_ref, wpa_ref, wpb_ref, o_ref, *, n_prompt_blocks):
    i = pl.program_id(0)

    def run(og, of):
        d = o_ref.shape[1]
        ma = gate_ref[:, :d].astype(F32)
        mb = gate_ref[:, d:].astype(F32)
        a = jnp.dot(og, wpa_ref[...], preferred_element_type=F32)
        b = jnp.dot(of, wpb_ref[...], preferred_element_type=F32)
        o_ref[...] = (jax.nn.sigmoid(ma) * a + jax.nn.sigmoid(mb) * b).astype(o_ref.dtype)

    @pl.when(i < n_prompt_blocks)
    def _():
        run(ogp_ref[...], ofp_ref[...])

    @pl.when(i >= n_prompt_blocks)
    def _():
        run(ogs_ref[...], ofs_ref[...])


def _merge(ogp, ogs, ofp, ofs, z3, wpa, wpb):
    tp, ts = ogp.shape[0], ogs.shape[0]
    d = wpa.shape[1]
    tb = ROW_BLOCK
    npb, nb = tp // tb, (tp + ts) // tb
    pidx = lambda i: (jnp.minimum(i, npb - 1), 0)
    sidx = lambda i: (jnp.maximum(i - npb, 0), 0)
    const = lambda i: (0, 0)
    return pl.pallas_call(
        functools.partial(_merge_kernel, n_prompt_blocks=npb),
        out_shape=jax.ShapeDtypeStruct((tp + ts, d), BF16),
        grid=(nb,),
        in_specs=[pl.BlockSpec((tb, GLA_V_W), pidx), pl.BlockSpec((tb, GLA_V_W), sidx),
                  pl.BlockSpec((tb, FOX_W), pidx), pl.BlockSpec((tb, FOX_W), sidx),
                  pl.BlockSpec((tb, 2 * d), lambda i: (i, 0)),
                  pl.BlockSpec(wpa.shape, const), pl.BlockSpec(wpb.shape, const)],
        out_specs=pl.BlockSpec((tb, d), lambda i: (i, 0)),
        compiler_params=_cparams(("parallel",)),
        name="merge",
    )(ogp, ogs, ofp, ofs, z3, wpa, wpb)


def _outproj_kernel(m_ref, w_ref, xp_ref, xs_ref, g1p_ref, g1s_ref, n2g_ref, scp_ref, shp_ref, scs_ref, shs_ref,
                    x1_ref, h2_ref, *, n_prompt_blocks):
    i = pl.program_id(0)
    y = jnp.dot(m_ref[...], w_ref[...], preferred_element_type=F32)

    def run(x, g1, sc, sh):
        x1 = x + g1 * y
        x1_ref[...] = x1
        h2_ref[...] = (_rms(x1, n2g_ref[...]) * (1.0 + sc) + sh).astype(h2_ref.dtype)

    @pl.when(i < n_prompt_blocks)
    def _():
        run(xp_ref[...], g1p_ref[...], scp_ref[...], shp_ref[...])

    @pl.when(i >= n_prompt_blocks)
    def _():
        run(xs_ref[...], g1s_ref[...], scs_ref[...], shs_ref[...])


def _outproj(merged, wout, xp, xs, g1p, g1s, n2g, scp, shp, scs, shs):
    tp, d = xp.shape
    ts = xs.shape[0]
    tb = ROW_BLOCK
    npb, nb = tp // tb, (tp + ts) // tb
    pidx = lambda i: (jnp.minimum(i, npb - 1), 0)
    sidx = lambda i: (jnp.maximum(i - npb, 0), 0)
    const = lambda i: (0, 0)
    row = pl.BlockSpec((1, d), const)
    srow = pl.BlockSpec((tb, d), sidx)
    return pl.pallas_call(
        functools.partial(_outproj_kernel, n_prompt_blocks=npb),
        out_shape=(jax.ShapeDtypeStruct((tp + ts, d), F32), jax.ShapeDtypeStruct((tp + ts, d), BF16)),
        grid=(nb,),
        in_specs=[pl.BlockSpec((tb, d), lambda i: (i, 0)), pl.BlockSpec(wout.shape, const),
                  pl.BlockSpec((tb, d), pidx), srow, row, srow, row, row, row, srow, srow],
        out_specs=(pl.BlockSpec((tb, d), lambda i: (i, 0)), pl.BlockSpec((tb, d), lambda i: (i, 0))),
        compiler_params=_cparams(("parallel",)),
        name="out_proj",
    )(merged, wout, xp, xs, g1p, g1s, n2g, scp, shp, scs, shs)


def _top_rows(x, n):
    out = []
    for _ in range(n):
        m = jnp.max(x, axis=0, keepdims=True)
        out.append(m)
        x = jnp.where(x == m, NEG_INF, x)
    return out


def _peer_topk_kernel(q_ref, sk_ref, th_ref, ka_ref, s2_ref, e2_ref):
    for h in range(PEER_HEADS):
        s = []
        for p in range(2):
            c0 = (2 * h + p) * PEER_NKEYS
            s.append(lax.dot_general(sk_ref[2 * h + p], q_ref[:, c0:c0 + PEER_NKEYS], NT,
                                     preferred_element_type=F32))
        v1 = _top_rows(s[0], PEER_TOPK)
        v2 = _top_rows(s[1], PEER_TOPK)
        cands = [v1[a] + v2[b] for a in range(PEER_TOPK) for b in range(PEER_TOPK)
                 if (a + 1) * (b + 1) <= PEER_TOPK]
        cands += [jnp.full_like(v1[0], NEG_INF)] * (-len(cands) % 8)
        sel = _top_rows(jnp.concatenate(cands, axis=0), PEER_TOPK)
        zsum = sel[0] * 0.0 + 1.0
        for r in range(1, PEER_TOPK):
            zsum = zsum + jnp.exp(sel[r] - sel[0])
        th_ref[h] = sel[PEER_TOPK - 1] - s[0]
        ka_ref[h] = jnp.exp(s[0] - v1[0]) / zsum
        s2_ref[h] = s[1]
        e2_ref[h] = jnp.exp(s[1] - v2[0])


def _peer_topk(q, sk):
    t = q.shape[0]
    tb = ROW_BLOCK
    shp = jax.ShapeDtypeStruct((PEER_HEADS, PEER_NKEYS, t), F32)
    ospec = pl.BlockSpec((PEER_HEADS, PEER_NKEYS, tb), lambda i: (0, 0, i))
    return pl.pallas_call(
        _peer_topk_kernel,
        out_shape=(shp, shp, shp, shp),
        grid=(t // tb,),
        in_specs=[pl.BlockSpec((tb, q.shape[1]), lambda i: (i, 0)),
                  pl.BlockSpec(sk.shape, lambda i: (0, 0, 0))],
        out_specs=(ospec, ospec, ospec, ospec),
        compiler_params=_cparams(("parallel",)),
        name="peer_topk",
    )(q, sk)


PEER_TOKEN_CHUNK = 256


def _peer_main_kernel(h2t_ref, u_ref, vt_ref, th_ref, ka_ref, s2_ref, e2_ref, o_ref):
    j = pl.program_id(1)
    te = u_ref.shape[0]
    tc = PEER_TOKEN_CHUNK
    n_chunks = h2t_ref.shape[1] // tc

    @pl.when(j == 0)
    def _():
        o_ref[...] = jnp.zeros_like(o_ref)

    def activations(c):
        return jnp.dot(u_ref[...], h2t_ref[:, c * tc:(c + 1) * tc], preferred_element_type=F32)

    def gated(c, at):
        cols = slice(c * tc, (c + 1) * tc)
        ws = []
        for a in range(te // PEER_NKEYS):
            x = at[a * PEER_NKEYS:(a + 1) * PEER_NKEYS]
            ge = 0.5 * x * (1.0 + lax.erf(x * (2.0 ** -0.5)))
            g = jnp.zeros_like(x)
            for h in range(PEER_HEADS):
                g = g + jnp.where(s2_ref[h, :, cols] >= th_ref[h, a:a + 1, cols],
                                  e2_ref[h, :, cols] * ka_ref[h, a:a + 1, cols], 0.0)
            ws.append((g * ge).astype(BF16))
        return jnp.concatenate(ws, axis=0)

    at = activations(0)
    for c in range(n_chunks):
        at_next = activations(c + 1) if c + 1 < n_chunks else None
        w = gated(c, at)
        o_ref[:, c * tc:(c + 1) * tc] += jnp.dot(vt_ref[...], w, preferred_element_type=F32)
        at = at_next


def _peer_main(h2t, u, vt, th, ka, s2, e2):
    d, t = h2t.shape
    n_exp = u.shape[0]
    tm = _pick(t, (768, 512, 256))
    te = 1024
    a_per = te // PEER_NKEYS
    sel_spec = pl.BlockSpec((PEER_HEADS, a_per, tm), lambda i, j: (0, j, i))
    tok_spec = pl.BlockSpec((PEER_HEADS, PEER_NKEYS, tm), lambda i, j: (0, 0, i))
    return pl.pallas_call(
        _peer_main_kernel,
        out_shape=jax.ShapeDtypeStruct((d, t), F32),
        grid=(t // tm, n_exp // te),
        in_specs=[pl.BlockSpec((d, tm), lambda i, j: (0, i)),
                  pl.BlockSpec((te, d), lambda i, j: (j, 0)),
                  pl.BlockSpec((d, te), lambda i, j: (0, j)),
                  sel_spec, sel_spec, tok_spec, tok_spec],
        out_specs=pl.BlockSpec((d, tm), lambda i, j: (0, i)),
        compiler_params=_cparams(("parallel", "arbitrary")),
        name="peer_main",
    )(h2t, u, vt, th, ka, s2, e2)


def _final_kernel(x1_ref, yt_ref, g2p_ref, g2s_ref, op_ref, os_ref, *, n_prompt_blocks):
    i = pl.program_id(0)
    y = yt_ref[...].T

    @pl.when(i < n_prompt_blocks)
    def _():
        op_ref[...] = x1_ref[...] + g2p_ref[...] * y

    @pl.when(i >= n_prompt_blocks)
    def _():
        os_ref[...] = x1_ref[...] + g2s_ref[...] * y


def _final(x1, yt, g2p, g2s, tp):
    t, d = x1.shape
    ts = t - tp
    tb = ROW_BLOCK
    npb, nb = tp // tb, t // tb
    pidx = lambda i: (jnp.minimum(i, npb - 1), 0)
    sidx = lambda i: (jnp.maximum(i - npb, 0), 0)
    return pl.pallas_call(
        functools.partial(_final_kernel, n_prompt_blocks=npb),
        out_shape=(jax.ShapeDtypeStruct((tp, d), F32), jax.ShapeDtypeStruct((ts, d), F32)),
        grid=(nb,),
        in_specs=[pl.BlockSpec((tb, d), lambda i: (i, 0)), pl.BlockSpec((d, tb), lambda i: (0, i)),
                  pl.BlockSpec((1, d), lambda i: (0, 0)), pl.BlockSpec((tb, d), sidx)],
        out_specs=(pl.BlockSpec((tb, d), pidx), pl.BlockSpec((tb, d), sidx)),
        compiler_params=_cparams(("arbitrary",)),
        name="final_residual",
    )(x1, yt, g2p, g2s)


def _layer(layer, xp, xs, c_prompt, c_sample, state_gla, cache_k, cache_v, cache_logf, page_table,
           norm1_g, norm2_g, w_ada, b_ada, w_in, gla_f2, gla_fb, gla_on_g, fox_qn_g, fox_kn_g, fox_fb,
           w_pa, w_pb, w_out, peer_wq, peer_subkeys, peer_u, peer_v):
    bp, tpp, d = xp.shape
    assert bp == 1 and d == D_MODEL
    b, n_new, _ = xs.shape
    tp, ts = bp * tpp, b * n_new
    assert tp % ROW_BLOCK == 0 and ts % ROW_BLOCK == 0
    xp2 = xp.reshape(tp, d)
    xs2 = xs.reshape(ts, d)

    n_c = 1 + b
    n_c_pad = -(-n_c // 8) * 8
    c_all = jnp.concatenate([c_prompt, c_sample, jnp.zeros((n_c_pad - n_c, d), F32)], axis=0)
    mod = _ada(c_all, w_ada, b_ada)
    mod_p = [mod[0:1, i * d:(i + 1) * d] for i in range(6)]
    mod_s = [jnp.repeat(mod[1:n_c, i * d:(i + 1) * d], n_new, axis=0) for i in range(6)]
    sh1p, sc1p, g1p, sh2p, sc2p, g2p = mod_p
    sh1s, sc1s, g1s, sh2s, sc2s, g2s = mod_s

    h = _normmod(xp2, xs2, norm1_g.reshape(1, d), sc1p, sh1p, sc1s, sh1s)

    w1 = jnp.concatenate([w_in[:, :_OFF_GF], w_in[:, _OFF_FQ:_OFF_FK]], axis=1).astype(BF16)
    w2 = w_in[:, _OFF_FK:_OFF_FF].astype(BF16)
    w3 = w_in[:, _OFF_MA:_OFF_END].astype(BF16)
    w4 = jnp.concatenate([w_in[:, _OFF_FF:_OFF_MA], w_in[:, _OFF_GF:_OFF_FQ],
                          jnp.zeros((d, LANES - FOX_HEADS - GLA_GATE_RANK), F32)], axis=1).astype(BF16)
    z1 = _matmul(h, w1, BF16, "in_proj_mixers")
    z2 = _matmul(h, w2, F32, "in_proj_fox_kv")
    z3 = _matmul(h, w3, BF16, "in_proj_gates")
    zs = _matmul(h, w4, F32, "in_proj_small")

    fb_pad = jnp.concatenate([fox_fb, jnp.zeros((LANES - FOX_HEADS,), F32)]).reshape(1, LANES)
    (qn, kb, vb, cum, k_p, v_p, lf_p, k_s, v_s, lf_s) = _fox_prep(
        z1, z2, zs, fox_qn_g.reshape(1, FOX_HD), fox_kn_g.reshape(1, FOX_HD), fb_pad, tp, n_new)
    neg_cum_t = (-LOG2E * cum[:tp, :FOX_HEADS]).T.reshape(FOX_HEADS, 1, tp)
    of_p = _fox_prompt(qn, kb, vb, neg_cum_t, tp)

    qs = qn[tp:].reshape(b, n_new, FOX_HEADS, FOX_HD)
    eye = jnp.eye(FOX_HEADS, dtype=BF16)
    qbd = (qs.transpose(0, 2, 3, 1)[:, :, :, None, :] * eye[None, :, None, :, None])
    qbd = qbd.reshape(b, FOX_W, FOX_HEADS * n_new)
    qbd = jnp.concatenate([qbd, jnp.zeros((b, FOX_W, LANES - FOX_HEADS * n_new), BF16)], axis=2)
    lfn_t = lf_s[:, :FOX_HEADS].reshape(b, n_new, FOX_HEADS).transpose(0, 2, 1)
    lfn_t = jnp.concatenate([lfn_t, jnp.zeros((b, FOX_HEADS, LANES - n_new), F32)], axis=2)
    of_s = _fox_sample(layer, page_table, qbd, k_s.reshape(b, n_new, FOX_W), v_s.reshape(b, n_new, FOX_W), lfn_t,
                       cache_k, cache_v, cache_logf).reshape(ts, FOX_W).astype(BF16)

    f2p = jnp.zeros((LANES, GLA_QK_W), F32).at[FOX_HEADS:FOX_HEADS + GLA_GATE_RANK].set(gla_f2)
    f2p = f2p.reshape(LANES, GLA_HEADS, GLA_DK).transpose(1, 0, 2).astype(BF16)
    fbh = gla_fb.reshape(GLA_HEADS, 1, GLA_DK)
    ong = gla_on_g.reshape(1, GLA_DV)
    og_p, sg_p = _gla_prompt(z1, zs, f2p, fbh, ong, jnp.zeros((GLA_HEADS, GLA_DK, GLA_DV), F32), tp)
    og_s, sg_s = _gla_sample(z1[tp:].astype(F32), zs[tp:], f2p, fbh, ong, state_gla, n_new)

    merged = _merge(og_p, og_s.astype(BF16), of_p, of_s, z3, w_pa.astype(BF16), w_pb.astype(BF16))
    x1, h2 = _outproj(merged, w_out.astype(BF16), xp2, xs2, g1p, g1s, norm2_g.reshape(1, d),
                      sc2p, sh2p, sc2s, sh2s)

    q = _matmul(h2, peer_wq.astype(BF16), BF16, "peer_query")
    sk = peer_subkeys.reshape(PEER_HEADS * 2, PEER_NKEYS, -1).astype(BF16)
    th, ka, s2, e2 = _peer_topk(q, sk)
    yt = _peer_main(h2.T, peer_u.astype(BF16), peer_v.astype(BF16).T, th, ka, s2, e2)
    out_p, out_s = _final(x1, yt, g2p, g2s, tp)

    return (out_p.reshape(bp, tpp, d), out_s.reshape(b, n_new, d),
            sg_p.reshape(bp, GLA_HEADS, GLA_DK, GLA_DV), sg_s,
            k_p.reshape(bp, tpp, FOX_HEADS, FOX_HD), v_p.reshape(bp, tpp, FOX_HEADS, FOX_HD),
            lf_p[:, :FOX_HEADS].reshape(bp, tpp, FOX_HEADS),
            k_s.reshape(b, n_new, FOX_HEADS, FOX_HD), v_s.reshape(b, n_new, FOX_HEADS, FOX_HD),
            lf_s[:, :FOX_HEADS].reshape(b, n_new, FOX_HEADS))


def kernel(x_prompt, x_sample, c_prompt, c_sample, state_gla, cache_k, cache_v, cache_logf, page_table,
           norm1_g, norm2_g, w_ada, b_ada, w_in, gla_f2, gla_fb, gla_on_g, fox_qn_g, fox_kn_g, fox_fb,
           w_pa, w_pb, w_out, peer_wq, peer_subkeys, peer_u, peer_v):
    depth = w_in.shape[0]
    xp, xs = x_prompt, x_sample
    outs = []
    for l in range(depth):
        res = _layer(l, xp, xs, c_prompt, c_sample, state_gla[l], cache_k, cache_v, cache_logf, page_table,
                     norm1_g[l], norm2_g[l], w_ada[l], b_ada[l], w_in[l], gla_f2[l], gla_fb[l], gla_on_g[l],
                     fox_qn_g[l], fox_kn_g[l], fox_fb[l], w_pa[l], w_pb[l], w_out[l],
                     peer_wq[l], peer_subkeys[l], peer_u[l], peer_v[l])
        xp, xs = res[0], res[1]
        outs.append(res[2:])
    stacked = [jnp.stack([o[i] for o in outs]) for i in range(8)]
    return (xp, xs, *stacked)
```

```python
import functools

import jax
import jax.numpy as jnp
from jax import lax
from jax.experimental import pallas as pl
from jax.experimental.pallas import tpu as pltpu

F32 = jnp.float32
BF16 = jnp.bfloat16

D_MODEL = 2048
GLA_HEADS = 4
GLA_DK = 128
GLA_DV = 256
GLA_GATE_RANK = 16
GLA_GATE_TAU = 16.0
GLA_CHUNK = 64
GLA_ROWBLOCK = 16
FOX_HEADS = 8
FOX_HD = 128
FOX_W = FOX_HEADS * FOX_HD
PEER_HEADS = 8
PEER_NKEYS = 128
PEER_TOPK = 16
PAGE_SIZE = 128
EPS = 1e-6

GLA_QK_W = GLA_HEADS * GLA_DK
GLA_V_W = GLA_HEADS * GLA_DV
_OFF_GF = 2 * GLA_QK_W + 2 * GLA_V_W
_OFF_FQ = _OFF_GF + GLA_GATE_RANK
_OFF_FK = _OFF_FQ + FOX_W
_OFF_FV = _OFF_FK + FOX_W
_OFF_FF = _OFF_FV + FOX_W
_OFF_MA = _OFF_FF + FOX_HEADS
_OFF_MB = _OFF_MA + D_MODEL
_OFF_END = _OFF_MB + D_MODEL

LANES = 128
ROW_BLOCK = 256
VMEM_LIMIT = 56 * 1024 * 1024

NT = (((1,), (1,)), ((), ()))
NEG_INF = float("-inf")
LOG2E = 1.4426950408889634


def _cparams(sem):
    return pltpu.CompilerParams(dimension_semantics=sem, vmem_limit_bytes=VMEM_LIMIT)


def _split3(x):
    hi = x.astype(BF16)
    r = x - hi.astype(F32)
    mid = r.astype(BF16)
    lo = (r - mid.astype(F32)).astype(BF16)
    return hi, mid, lo


def _dot01(m01, x):
    acc = None
    for p in _split3(x):
        t = jnp.dot(m01, p, preferred_element_type=F32)
        acc = t if acc is None else acc + t
    return acc


def _dotx01(x, m01):
    acc = None
    for p in _split3(x):
        t = jnp.dot(p, m01, preferred_element_type=F32)
        acc = t if acc is None else acc + t
    return acc


def _log_sigmoid(z):
    return jnp.minimum(z, 0.0) - jnp.log1p(jnp.exp(-jnp.abs(z)))


def _rms(x, g):
    return x * lax.rsqrt(jnp.mean(x * x, axis=-1, keepdims=True) + EPS) * g


def _ada_kernel(c_ref, w_ref, b_ref, o_ref):
    o_ref[...] = jnp.dot(c_ref[...].astype(BF16), w_ref[...].astype(BF16),
                         preferred_element_type=F32) + b_ref[...]


def _ada(c_all, w_ada, b_ada):
    m, k = c_all.shape
    n = w_ada.shape[1]
    tn = 1024
    return pl.pallas_call(
        _ada_kernel,
        out_shape=jax.ShapeDtypeStruct((m, n), F32),
        grid=(n // tn,),
        in_specs=[pl.BlockSpec((m, k), lambda j: (0, 0)),
                  pl.BlockSpec((k, tn), lambda j: (0, j)),
                  pl.BlockSpec((1, tn), lambda j: (0, j))],
        out_specs=pl.BlockSpec((m, tn), lambda j: (0, j)),
        compiler_params=_cparams(("parallel",)),
        name="ada_mod",
    )(c_all, w_ada, b_ada.reshape(1, n))


def _normmod_kernel(xp_ref, xs_ref, g_ref, scp_ref, shp_ref, scs_ref, shs_ref, o_ref, *, n_prompt_blocks):
    i = pl.program_id(0)

    @pl.when(i < n_prompt_blocks)
    def _():
        h = _rms(xp_ref[...], g_ref[...]) * (1.0 + scp_ref[...]) + shp_ref[...]
        o_ref[...] = h.astype(o_ref.dtype)

    @pl.when(i >= n_prompt_blocks)
    def _():
        h = _rms(xs_ref[...], g_ref[...]) * (1.0 + scs_ref[...]) + shs_ref[...]
        o_ref[...] = h.astype(o_ref.dtype)


def _normmod(xp, xs, g, scp, shp, scs, shs):
    tp, d = xp.shape
    ts = xs.shape[0]
    tb = ROW_BLOCK
    npb, nsb = tp // tb, ts // tb
    pidx = lambda i: (jnp.minimum(i, npb - 1), 0)
    sidx = lambda i: (jnp.maximum(i - npb, 0), 0)
    row = pl.BlockSpec((1, d), lambda i: (0, 0))
    return pl.pallas_call(
        functools.partial(_normmod_kernel, n_prompt_blocks=npb),
        out_shape=jax.ShapeDtypeStruct((tp + ts, d), BF16),
        grid=(npb + nsb,),
        in_specs=[pl.BlockSpec((tb, d), pidx), pl.BlockSpec((tb, d), sidx), row, row, row,
                  pl.BlockSpec((tb, d), sidx), pl.BlockSpec((tb, d), sidx)],
        out_specs=pl.BlockSpec((tb, d), lambda i: (i, 0)),
        compiler_params=_cparams(("parallel",)),
        name="norm_mod",
    )(xp, xs, g, scp, shp, scs, shs)


def _mm_kernel(a_ref, w_ref, o_ref):
    o_ref[...] = jnp.dot(a_ref[...], w_ref[...], preferred_element_type=F32).astype(o_ref.dtype)


def _pick(n, cands):
    for c in cands:
        if n % c == 0:
            return c
    return n


def _matmul(a, w, out_dtype, name):
    m, k = a.shape
    n = w.shape[1]
    tm = _pick(m, (1024, 768, 512, 256))
    tn = _pick(n, (1024, 512, 256, 128))
    return pl.pallas_call(
        _mm_kernel,
        out_shape=jax.ShapeDtypeStruct((m, n), out_dtype),
        grid=(m // tm, n // tn),
        in_specs=[pl.BlockSpec((tm, k), lambda i, j: (i, 0)),
                  pl.BlockSpec((k, tn), lambda i, j: (0, j))],
        out_specs=pl.BlockSpec((tm, tn), lambda i, j: (i, j)),
        compiler_params=_cparams(("parallel", "parallel")),
        name=name,
    )(a, w)


def _fox_prep_kernel(fq_ref, fkv_ref, zs_ref, qg_ref, kg_ref, fb_ref,
                     qn_ref, kb_ref, vb_ref, cum_ref, kp_ref, vp_ref, lfp_ref, ks_ref, vs_ref, lfs_ref,
                     carry_ref, *, n_prompt_blocks, seg_len):
    i = pl.program_id(0)
    tb = fq_ref.shape[0]
    is_prompt = i < n_prompt_blocks

    @pl.when(i == 0)
    def _():
        carry_ref[...] = jnp.zeros_like(carry_ref)

    scale = FOX_HD ** -0.5 * jnp.where(is_prompt, LOG2E, 1.0).astype(F32)
    fq = fq_ref[...].astype(F32)
    fkv = fkv_ref[...]
    kn = []
    for h in range(FOX_HEADS):
        sl = slice(h * FOX_HD, (h + 1) * FOX_HD)
        qn_ref[:, sl] = (_rms(fq[:, sl], qg_ref[...]) * scale).astype(qn_ref.dtype)
        kn.append(_rms(fkv[:, sl], kg_ref[...]))
    kn = jnp.concatenate(kn, axis=1)
    v = fkv[:, FOX_W:]
    kb_ref[...] = kn.astype(kb_ref.dtype)
    vb_ref[...] = v.astype(vb_ref.dtype)

    lf = _log_sigmoid(zs_ref[...] + fb_ref[...])
    row = lax.broadcasted_iota(jnp.int32, (tb, tb), 0)
    col = lax.broadcasted_iota(jnp.int32, (tb, tb), 1)

    @pl.when(is_prompt)
    def _():
        tri = jnp.where(col <= row, 1.0, 0.0).astype(BF16)
        cum = _dot01(tri, lf) + carry_ref[...]
        carry_ref[...] = cum[tb - 1:tb, :]
        cum_ref[...] = cum
        kp_ref[...] = kn
        vp_ref[...] = v
        lfp_ref[...] = lf

    @pl.when(jnp.logical_not(is_prompt))
    def _():
        tri = jnp.where(col <= row, jnp.where((row // seg_len) == (col // seg_len), 1.0, 0.0), 0.0).astype(BF16)
        cum_ref[...] = _dot01(tri, lf)
        ks_ref[...] = kn
        vs_ref[...] = v
        lfs_ref[...] = lf


def _fox_prep(z1, z2, zs, qg, kg, fb_pad, tp, seg_len):
    t_all = z1.shape[0]
    ts = t_all - tp
    tb = ROW_BLOCK
    npb, nb = tp // tb, t_all // tb
    pidx = lambda i: (jnp.minimum(i, npb - 1), 0)
    sidx = lambda i: (jnp.maximum(i - npb, 0), 0)
    allidx = lambda i: (i, 0)
    row = pl.BlockSpec((1, LANES), lambda i: (0, 0))
    fq_block = (_OFF_GF) // FOX_W
    return pl.pallas_call(
        functools.partial(_fox_prep_kernel, n_prompt_blocks=npb, seg_len=seg_len),
        out_shape=(jax.ShapeDtypeStruct((t_all, FOX_W), BF16),
                   jax.ShapeDtypeStruct((t_all, FOX_W), BF16),
                   jax.ShapeDtypeStruct((t_all, FOX_W), BF16),
                   jax.ShapeDtypeStruct((t_all, LANES), F32),
                   jax.ShapeDtypeStruct((tp, FOX_W), F32),
                   jax.ShapeDtypeStruct((tp, FOX_W), F32),
                   jax.ShapeDtypeStruct((tp, LANES), F32),
                   jax.ShapeDtypeStruct((ts, FOX_W), F32),
                   jax.ShapeDtypeStruct((ts, FOX_W), F32),
                   jax.ShapeDtypeStruct((ts, LANES), F32)),
        grid=(nb,),
        in_specs=[pl.BlockSpec((tb, FOX_W), lambda i: (i, fq_block)),
                  pl.BlockSpec((tb, 2 * FOX_W), allidx),
                  pl.BlockSpec((tb, LANES), allidx), row, row, row],
        out_specs=(pl.BlockSpec((tb, FOX_W), allidx), pl.BlockSpec((tb, FOX_W), allidx),
                   pl.BlockSpec((tb, FOX_W), allidx), pl.BlockSpec((tb, LANES), allidx),
                   pl.BlockSpec((tb, FOX_W), pidx), pl.BlockSpec((tb, FOX_W), pidx),
                   pl.BlockSpec((tb, LANES), pidx),
                   pl.BlockSpec((tb, FOX_W), sidx), pl.BlockSpec((tb, FOX_W), sidx),
                   pl.BlockSpec((tb, LANES), sidx)),
        scratch_shapes=[pltpu.VMEM((1, LANES), F32)],
        compiler_params=_cparams(("arbitrary",)),
        name="fox_prep",
    )(z1, z2, zs, qg, kg, fb_pad)


FOX_HEADS_PER_STEP = 2


def _fox_prompt_kernel(q_ref, k_ref, v_ref, nc_ref, o_ref, *, tk):
    qi = pl.program_id(1)
    tq = q_ref.shape[0]
    nh = q_ref.shape[1] // FOX_HD
    n_full = (qi * tq) // tk
    n_diag = -(-tq // tk)

    def block(c0, masked, carry):
        out = []
        for h in range(nh):
            m, l, acc = carry[h]
            sl = slice(h * FOX_HD, (h + 1) * FOX_HD)
            s = lax.dot_general(q_ref[:, sl], k_ref[pl.ds(c0, tk), sl], NT, preferred_element_type=F32)
            s = s + nc_ref[h, :, pl.ds(c0, tk)]
            if masked:
                row = qi * tq + lax.broadcasted_iota(jnp.int32, (tq, tk), 0)
                col = c0 + lax.broadcasted_iota(jnp.int32, (tq, tk), 1)
                s = jnp.where(col <= row, s, NEG_INF)
            m_new = jnp.maximum(m, jnp.max(s, axis=1, keepdims=True))
            alpha = jnp.exp2(m - m_new)
            p = jnp.exp2(s - m_new)
            l = alpha * l + jnp.sum(p, axis=1, keepdims=True)
            acc = alpha * acc + jnp.dot(p.astype(BF16), v_ref[pl.ds(c0, tk), sl], preferred_element_type=F32)
            out.append((m_new, l, acc))
        return tuple(out)

    init = tuple((jnp.full((tq, 1), NEG_INF, F32), jnp.zeros((tq, 1), F32), jnp.zeros((tq, FOX_HD), F32))
                 for _ in range(nh))
    carry = lax.fori_loop(0, n_full, lambda i, c: block(pl.multiple_of(i * tk, tk), False, c), init)
    final = lax.fori_loop(n_full, n_full + n_diag, lambda i, c: block(pl.multiple_of(i * tk, tk), True, c), carry)
    for h in range(nh):
        _, l, acc = final[h]
        o_ref[:, h * FOX_HD:(h + 1) * FOX_HD] = (acc / l).astype(o_ref.dtype)


def _fox_prompt(qn, kb, vb, neg_cum_t, tp):
    tk = _pick(tp, (512, 256, 128))
    tq = tk
    nh = FOX_HEADS_PER_STEP
    w = nh * FOX_HD
    return pl.pallas_call(
        functools.partial(_fox_prompt_kernel, tk=tk),
        out_shape=jax.ShapeDtypeStruct((tp, FOX_W), BF16),
        grid=(FOX_HEADS // nh, tp // tq),
        in_specs=[pl.BlockSpec((tq, w), lambda g, i: (i, g)),
                  pl.BlockSpec((tp, w), lambda g, i: (0, g)),
                  pl.BlockSpec((tp, w), lambda g, i: (0, g)),
                  pl.BlockSpec((nh, 1, tp), lambda g, i: (g, 0, 0))],
        out_specs=pl.BlockSpec((tq, w), lambda g, i: (i, g)),
        compiler_params=_cparams(("parallel", "parallel")),
        name="fox_prompt",
    )(qn, kb, vb, neg_cum_t)


def _expand_heads(x_t, n_q):
    hi, mid, lo = _split3(x_t)
    n = x_t.shape[0]
    stacked = jnp.concatenate([hi.astype(F32), mid.astype(F32), lo.astype(F32),
                               jnp.zeros((LANES - 3 * n, LANES), F32)], axis=0)
    k = lax.broadcasted_iota(jnp.int32, (LANES, LANES), 0)
    c = lax.broadcasted_iota(jnp.int32, (LANES, LANES), 1)
    e3 = jnp.where(k < 3 * n, jnp.where(c < n * n_q, jnp.where((k % n) == (c // n_q), 1.0, 0.0), 0.0), 0.0)
    return jnp.dot(stacked.T.astype(BF16), e3.astype(BF16), preferred_element_type=F32)


def _page_to_rows(page_ref, buf_ref, p):
    for h in range(FOX_HEADS):
        x = page_ref[pl.ds(h, PAGE_SIZE, stride=FOX_HEADS), :]
        buf_ref[p * PAGE_SIZE:(p + 1) * PAGE_SIZE, h * FOX_HD:(h + 1) * FOX_HD] = x.astype(BF16)


def _fox_sample_kernel(pt_ref, qbd_ref, kn_ref, vn_ref, lfn_ref, *rest, pages_per_step, n_groups):
    pg = pages_per_step
    k_refs = rest[:pg]
    lf_refs = rest[pg:2 * pg]
    v_refs = rest[2 * pg:3 * pg]
    o_ref = rest[3 * pg]
    s_all, s_new, m_ref, l_ref, acc_ref, carry_ref, kv_buf, eall_ref = rest[3 * pg + 1:]
    j = pl.program_id(1)
    n_new = kn_ref.shape[0]
    hp = FOX_HEADS
    rows = pg * PAGE_SIZE
    ki = lax.broadcasted_iota(jnp.int32, (LANES, LANES), 0)
    ci = lax.broadcasted_iota(jnp.int32, (LANES, LANES), 1)

    @pl.when(j == 0)
    def _():
        pad = jnp.zeros((LANES - n_new, FOX_W), F32)
        knp = jnp.concatenate([kn_ref[...], pad], axis=0).astype(BF16)
        incl = jnp.where(ki <= ci, 1.0, 0.0).astype(BF16)
        cn_t = _dotx01(lfn_ref[...], incl)
        s = jnp.dot(knp, qbd_ref[...], preferred_element_type=F32) - _expand_heads(cn_t, n_new)
        s = jnp.where(ki <= ci % n_new, s, NEG_INF)
        s = jnp.where(ki < n_new, s, NEG_INF)
        s = jnp.where(ci < FOX_HEADS * n_new, s, NEG_INF)
        s_new[...] = s
        m_ref[...] = jnp.max(s, axis=0, keepdims=True)
        carry_ref[...] = jnp.zeros_like(carry_ref)
        kk = lax.broadcasted_iota(jnp.int32, eall_ref.shape, 0)
        cc = lax.broadcasted_iota(jnp.int32, eall_ref.shape, 1)
        hit = kk == (cc // LANES) * hp + (cc % LANES) // n_new
        eall_ref[...] = jnp.where(hit, jnp.where(cc % LANES < FOX_HEADS * n_new, 1.0, 0.0), 0.0).astype(BF16)

    @pl.when(j < n_groups)
    def _():
        g = n_groups - 1 - j
        for p in range(pg):
            _page_to_rows(k_refs[p], kv_buf, p)
        s = jnp.dot(kv_buf[...], qbd_ref[...], preferred_element_type=F32)
        tiles = [lf_refs[p][...] for p in range(pg)]
        if pg * hp < LANES:
            tiles.append(jnp.zeros((LANES - pg * hp, PAGE_SIZE), F32))
        x = jnp.concatenate(tiles, axis=0).T
        later_t = jnp.where(ci > ki, 1.0, 0.0).astype(BF16)
        same_head = (ki % hp) == (ci % hp)
        later_page = jnp.where(same_head, jnp.where(ki // hp > ci // hp, 1.0, 0.0), 0.0).astype(BF16)
        any_page = jnp.where(same_head, 1.0, 0.0).astype(BF16)
        tot = jnp.broadcast_to(jnp.sum(x, axis=0, keepdims=True), (8, LANES))
        tail = _dot01(later_t, x) + (_dotx01(tot, later_page)[0:1] + carry_ref[...])
        carry_ref[...] = carry_ref[...] + _dotx01(tot, any_page)[0:1]
        ex = jnp.dot(jnp.concatenate(list(_split3(tail)), axis=0), eall_ref[...], preferred_element_type=F32)
        blocks = []
        for p in range(pg):
            cols = slice(p * LANES, (p + 1) * LANES)
            add = ex[0:LANES, cols] + ex[LANES:2 * LANES, cols] + ex[2 * LANES:3 * LANES, cols]
            blocks.append(s[p * PAGE_SIZE:(p + 1) * PAGE_SIZE] + add)
        s = jnp.concatenate(blocks, axis=0)
        s_all[pl.ds(pl.multiple_of(g * rows, rows), rows), :] = s
        m_ref[...] = jnp.maximum(m_ref[...], jnp.max(s, axis=0, keepdims=True))

    @pl.when(j == n_groups)
    def _():
        pad = jnp.zeros((LANES - n_new, FOX_W), F32)
        vnp = jnp.concatenate([vn_ref[...], pad], axis=0).astype(BF16)
        p = jnp.exp(s_new[...] - m_ref[...])
        l_ref[...] = jnp.sum(p, axis=0, keepdims=True)
        acc_ref[...] = jnp.dot(p.T.astype(BF16), vnp, preferred_element_type=F32)

    @pl.when(j >= n_groups)
    def _():
        g = 2 * n_groups - 1 - j
        for p in range(pg):
            _page_to_rows(v_refs[p], kv_buf, p)
        pexp = jnp.exp(s_all[pl.ds(pl.multiple_of(g * rows, rows), rows), :] - m_ref[...])
        l_ref[...] += jnp.sum(pexp, axis=0, keepdims=True)
        acc_ref[...] += jnp.dot(pexp.T.astype(BF16), kv_buf[...], preferred_element_type=F32)

    @pl.when(j == 2 * n_groups - 1)
    def _():
        l_col = jnp.broadcast_to(l_ref[...], (LANES, LANES)).T
        for h in range(FOX_HEADS):
            rows = slice(h * n_new, (h + 1) * n_new)
            cols = slice(h * FOX_HD, (h + 1) * FOX_HD)
            o_ref[:, cols] = acc_ref[rows, cols] / l_col[rows, :]


def _fox_sample(layer, page_table, qbd, kn, vn, lfn_t, ck, cv, clf):
    ck = ck.reshape(ck.shape[0], ck.shape[1], PAGE_SIZE * FOX_HEADS, FOX_HD)
    cv = cv.reshape(cv.shape[0], cv.shape[1], PAGE_SIZE * FOX_HEADS, FOX_HD)
    clf_t = clf.transpose(0, 1, 3, 2)
    b, n_pages = page_table.shape
    n_new = kn.shape[1]
    pg = _pick(n_pages, (16, 8, 4, 2))
    ng = n_pages // pg
    assert FOX_HEADS * pg <= LANES and FOX_HEADS * n_new <= LANES

    def page_map(r, phase_v, n_trailing):
        def index(bi, j, pt):
            step = jnp.maximum(j - ng, 0) if phase_v else jnp.minimum(j, ng - 1)
            return (layer, pt[bi, (ng - 1 - step) * pg + r]) + (0,) * n_trailing
        return index

    per_b = lambda bi, j, pt: (bi, 0, 0)
    page_blk = (None, None, PAGE_SIZE * FOX_HEADS, FOX_HD)
    in_specs = [pl.BlockSpec((None, FOX_W, LANES), per_b),
                pl.BlockSpec((None, n_new, FOX_W), per_b),
                pl.BlockSpec((None, n_new, FOX_W), per_b),
                pl.BlockSpec((None, FOX_HEADS, LANES), per_b)]
    in_specs += [pl.BlockSpec(page_blk, page_map(r, False, 2)) for r in range(pg)]
    in_specs += [pl.BlockSpec((None, None, FOX_HEADS, PAGE_SIZE), page_map(r, False, 2)) for r in range(pg)]
    in_specs += [pl.BlockSpec(page_blk, page_map(r, True, 2)) for r in range(pg)]
    grid_spec = pltpu.PrefetchScalarGridSpec(
        num_scalar_prefetch=1,
        grid=(b, 2 * ng),
        in_specs=in_specs,
        out_specs=pl.BlockSpec((None, n_new, FOX_W), per_b),
        scratch_shapes=[pltpu.VMEM((n_pages * PAGE_SIZE, LANES), F32),
                        pltpu.VMEM((LANES, LANES), F32),
                        pltpu.VMEM((1, LANES), F32),
                        pltpu.VMEM((1, LANES), F32),
                        pltpu.VMEM((LANES, FOX_W), F32),
                        pltpu.VMEM((1, LANES), F32),
                        pltpu.VMEM((pg * PAGE_SIZE, FOX_W), BF16),
                        pltpu.VMEM((LANES, pg * LANES), BF16)],
    )
    return pl.pallas_call(
        functools.partial(_fox_sample_kernel, pages_per_step=pg, n_groups=ng),
        out_shape=jax.ShapeDtypeStruct((b, n_new, FOX_W), F32),
        grid_spec=grid_spec,
        compiler_params=_cparams(("parallel", "arbitrary")),
        name="fox_sample",
    )(page_table, qbd, kn, vn, lfn_t, *([ck] * pg), *([clf_t] * pg), *([cv] * pg))


def _gla_chunk(q, k, v, zs, f2p, fb, s_state, n_row_blocks, rb):
    c = q.shape[0]
    z = jnp.dot(zs.astype(BF16), f2p, preferred_element_type=F32) + fb
    la = _log_sigmoid(z) * (1.0 / GLA_GATE_TAU)
    ri = lax.broadcasted_iota(jnp.int32, (c, c), 0)
    ci = lax.broadcasted_iota(jnp.int32, (c, c), 1)
    b = _dot01(jnp.where(ci <= ri, 1.0, 0.0).astype(BF16), la)
    bl = b[c - 1:c]
    rbi = lax.broadcasted_iota(jnp.int32, (rb, c), 0)
    cbi = lax.broadcasted_iota(jnp.int32, (rb, c), 1)
    rows = []
    for i in range(n_row_blocks):
        r0 = i * rb
        qb, kb_, bb = q[r0:r0 + rb], k[r0:r0 + rb], b[r0:r0 + rb]
        a = jnp.zeros((rb, c), F32)
        if i > 0:
            bref = b[r0:r0 + 1]
            qt = qb * jnp.exp(bb - bref)
            kt = k * jnp.exp(jnp.minimum(bref - b, 0.0))
            a = lax.dot_general(qt.astype(BF16), kt.astype(BF16), NT, preferred_element_type=F32)
            a = jnp.where(cbi < r0, a, 0.0)
        for s in range(rb):
            e = jnp.exp(jnp.minimum(bb - bb[s:s + 1], 0.0))
            colv = jnp.sum(qb * kb_[s:s + 1] * e, axis=-1, keepdims=True)
            a = jnp.where(cbi == r0 + s, jnp.where(rbi >= s, colv, 0.0), a)
        rows.append(a)
    a = rows[0] if len(rows) == 1 else jnp.concatenate(rows, axis=0)
    nr = n_row_blocks * rb
    o = (jnp.dot(a.astype(BF16), v, preferred_element_type=F32)
         + jnp.dot((q[:nr] * jnp.exp(b[:nr])).astype(BF16), s_state.astype(BF16), preferred_element_type=F32))
    kh = (k * jnp.exp(bl - b)).astype(BF16)
    blc = jnp.exp(jnp.broadcast_to(bl, (GLA_DK, GLA_DK)).T)
    upd = lax.dot_general(kh, v, (((0,), (0,)), ((), ())), preferred_element_type=F32)
    s_next = jnp.concatenate([blc] * (GLA_DV // GLA_DK), axis=1) * s_state + upd
    return o, s_next


def _gla_out(o, gr, ong):
    return _rms(o, ong) * (gr * jax.nn.sigmoid(gr))


def _gla_prompt_kernel(q_ref, k_ref, v_ref, gr_ref, zs_ref, f2p_ref, fb_ref, ong_ref, s0_ref,
                       og_ref, sout_ref, s_ref):
    i = pl.program_id(0)
    tb = q_ref.shape[0]
    c = GLA_CHUNK

    @pl.when(i == 0)
    def _():
        s_ref[...] = s0_ref[...]

    def chunk(n, carry):
        r = pl.multiple_of(n * c, c)
        zs = zs_ref[pl.ds(r, c), :]
        for h in range(GLA_HEADS):
            qk = slice(h * GLA_DK, (h + 1) * GLA_DK)
            vv = slice(h * GLA_DV, (h + 1) * GLA_DV)
            q = q_ref[pl.ds(r, c), qk].astype(F32) * (GLA_DK ** -0.5)
            k = k_ref[pl.ds(r, c), qk].astype(F32)
            o, s_next = _gla_chunk(q, k, v_ref[pl.ds(r, c), vv], zs, f2p_ref[h], fb_ref[h],
                                   s_ref[h], c // GLA_ROWBLOCK, GLA_ROWBLOCK)
            s_ref[h] = s_next
            og_ref[pl.ds(r, c), vv] = _gla_out(o, gr_ref[pl.ds(r, c), vv].astype(F32),
                                               ong_ref[...]).astype(og_ref.dtype)
        return carry

    lax.fori_loop(0, tb // c, chunk, 0)

    @pl.when(i == pl.num_programs(0) - 1)
    def _():
        sout_ref[...] = s_ref[...]


def _gla_prompt(z1, zs, f2p, fb, ong, s0, tp):
    tb = _pick(tp, (512, 256, 128, 64))
    whole = lambda i: (0, 0, 0)
    return pl.pallas_call(
        _gla_prompt_kernel,
        out_shape=(jax.ShapeDtypeStruct((tp, GLA_V_W), BF16),
                   jax.ShapeDtypeStruct((GLA_HEADS, GLA_DK, GLA_DV), F32)),
        grid=(tp // tb,),
        in_specs=[pl.BlockSpec((tb, GLA_QK_W), lambda i: (i, 0)),
                  pl.BlockSpec((tb, GLA_QK_W), lambda i: (i, 1)),
                  pl.BlockSpec((tb, GLA_V_W), lambda i: (i, 1)),
                  pl.BlockSpec((tb, GLA_V_W), lambda i: (i, 2)),
                  pl.BlockSpec((tb, LANES), lambda i: (i, 0)),
                  pl.BlockSpec((GLA_HEADS, LANES, GLA_DK), whole),
                  pl.BlockSpec((GLA_HEADS, 1, GLA_DK), whole),
                  pl.BlockSpec((1, GLA_DV), lambda i: (0, 0)),
                  pl.BlockSpec((GLA_HEADS, GLA_DK, GLA_DV), whole)],
        out_specs=(pl.BlockSpec((tb, GLA_V_W), lambda i: (i, 0)),
                   pl.BlockSpec((GLA_HEADS, GLA_DK, GLA_DV), whole)),
        scratch_shapes=[pltpu.VMEM((GLA_HEADS, GLA_DK, GLA_DV), F32)],
        compiler_params=_cparams(("arbitrary",)),
        name="gla_prompt",
    )(z1, z1, z1, z1, zs, f2p, fb, ong, s0)


def _gla_sample_kernel(q_ref, k_ref, v_ref, gr_ref, zs_ref, f2p_ref, fb_ref, ong_ref, s0_ref, og_ref, sout_ref):
    t = q_ref.shape[0]
    pad = lambda x: jnp.concatenate([x, jnp.zeros((LANES - t, x.shape[1]), x.dtype)], axis=0)
    zs = pad(zs_ref[...])
    rowi = lax.broadcasted_iota(jnp.int32, (LANES, GLA_DK), 0)
    for h in range(GLA_HEADS):
        qk = slice(h * GLA_DK, (h + 1) * GLA_DK)
        vv = slice(h * GLA_DV, (h + 1) * GLA_DV)
        q = pad(q_ref[:, qk] * (GLA_DK ** -0.5))
        k = pad(k_ref[:, qk])
        v = pad(v_ref[:, vv]).astype(BF16)
        fb = jnp.where(rowi < t, fb_ref[h], 1e4)
        o, s_next = _gla_chunk(q, k, v, zs, f2p_ref[h], fb, s0_ref[h], 1, t)
        og_ref[:, vv] = _gla_out(o, gr_ref[:, vv], ong_ref[...])
        sout_ref[h] = s_next


def _gla_sample(z1s, zss, f2p, fb, ong, s0, n_new):
    ts = z1s.shape[0]
    b = ts // n_new
    whole = lambda bi: (0, 0, 0)
    state_spec = pl.BlockSpec((None, GLA_HEADS, GLA_DK, GLA_DV), lambda bi: (bi, 0, 0, 0))
    return pl.pallas_call(
        _gla_sample_kernel,
        out_shape=(jax.ShapeDtypeStruct((ts, GLA_V_W), F32),
                   jax.ShapeDtypeStruct((b, GLA_HEADS, GLA_DK, GLA_DV), F32)),
        grid=(b,),
        in_specs=[pl.BlockSpec((n_new, GLA_QK_W), lambda bi: (bi, 0)),
                  pl.BlockSpec((n_new, GLA_QK_W), lambda bi: (bi, 1)),
                  pl.BlockSpec((n_new, GLA_V_W), lambda bi: (bi, 1)),
                  pl.BlockSpec((n_new, GLA_V_W), lambda bi: (bi, 2)),
                  pl.BlockSpec((n_new, LANES), lambda bi: (bi, 0)),
                  pl.BlockSpec((GLA_HEADS, LANES, GLA_DK), whole),
                  pl.BlockSpec((GLA_HEADS, 1, GLA_DK), whole),
                  pl.BlockSpec((1, GLA_DV), lambda bi: (0, 0)),
                  state_spec],
        out_specs=(pl.BlockSpec((n_new, GLA_V_W), lambda bi: (bi, 0)), state_spec),
        compiler_params=_cparams(("parallel",)),
        name="gla_sample",
    )(z1s, z1s, z1s, z1s, zss, f2p, fb, ong, s0)


def _merge_kernel(ogp_ref, ogs_ref, ofp_ref, ofs_ref, gate_ref, wpa_ref, wpb_ref, o_ref, *, n_prompt_blocks):
    i = pl.program_id(0)

    def run(og, of):
        d = o_ref.shape[1]
        ma = gate_ref[:, :d].astype(F32)
        mb = gate_ref[:, d:].astype(F32)
        a = jnp.dot(og, wpa_ref[...], preferred_element_type=F32)
        b = jnp.dot(of, wpb_ref[...], preferred_element_type=F32)
        o_ref[...] = (jax.nn.sigmoid(ma) * a + jax.nn.sigmoid(mb) * b).astype(o_ref.dtype)

    @pl.when(i < n_prompt_blocks)
    def _():
        run(ogp_ref[...], ofp_ref[...])

    @pl.when(i >= n_prompt_blocks)
    def _():
        run(ogs_ref[...], ofs_ref[...])


def _merge(ogp, ogs, ofp, ofs, z3, wpa, wpb):
    tp, ts = ogp.shape[0], ogs.shape[0]
    d = wpa.shape[1]
    tb = ROW_BLOCK
    npb, nb = tp // tb, (tp + ts) // tb
    pidx = lambda i: (jnp.minimum(i, npb - 1), 0)
    sidx = lambda i: (jnp.maximum(i - npb, 0), 0)
    const = lambda i: (0, 0)
    return pl.pallas_call(
        functools.partial(_merge_kernel, n_prompt_blocks=npb),
        out_shape=jax.ShapeDtypeStruct((tp + ts, d), BF16),
        grid=(nb,),
        in_specs=[pl.BlockSpec((tb, GLA_V_W), pidx), pl.BlockSpec((tb, GLA_V_W), sidx),
                  pl.BlockSpec((tb, FOX_W), pidx), pl.BlockSpec((tb, FOX_W), sidx),
                  pl.BlockSpec((tb, 2 * d), lambda i: (i, 0)),
                  pl.BlockSpec(wpa.shape, const), pl.BlockSpec(wpb.shape, const)],
        out_specs=pl.BlockSpec((tb, d), lambda i: (i, 0)),
        compiler_params=_cparams(("parallel",)),
        name="merge",
    )(ogp, ogs, ofp, ofs, z3, wpa, wpb)


def _outproj_kernel(m_ref, w_ref, xp_ref, xs_ref, g1p_ref, g1s_ref, n2g_ref, scp_ref, shp_ref, scs_ref, shs_ref,
                    x1_ref, h2_ref, *, n_prompt_blocks):
    i = pl.program_id(0)
    y = jnp.dot(m_ref[...], w_ref[...], preferred_element_type=F32)

    def run(x, g1, sc, sh):
        x1 = x + g1 * y
        x1_ref[...] = x1
        h2_ref[...] = (_rms(x1, n2g_ref[...]) * (1.0 + sc) + sh).astype(h2_ref.dtype)

    @pl.when(i < n_prompt_blocks)
    def _():
        run(xp_ref[...], g1p_ref[...], scp_ref[...], shp_ref[...])

    @pl.when(i >= n_prompt_blocks)
    def _():
        run(xs_ref[...], g1s_ref[...], scs_ref[...], shs_ref[...])


def _outproj(merged, wout, xp, xs, g1p, g1s, n2g, scp, shp, scs, shs):
    tp, d = xp.shape
    ts = xs.shape[0]
    tb = ROW_BLOCK
    npb, nb = tp // tb, (tp + ts) // tb
    pidx = lambda i: (jnp.minimum(i, npb - 1), 0)
    sidx = lambda i: (jnp.maximum(i - npb, 0), 0)
    const = lambda i: (0, 0)
    row = pl.BlockSpec((1, d), const)
    srow = pl.BlockSpec((tb, d), sidx)
    return pl.pallas_call(
        functools.partial(_outproj_kernel, n_prompt_blocks=npb),
        out_shape=(jax.ShapeDtypeStruct((tp + ts, d), F32), jax.ShapeDtypeStruct((tp + ts, d), BF16)),
        grid=(nb,),
        in_specs=[pl.BlockSpec((tb, d), lambda i: (i, 0)), pl.BlockSpec(wout.shape, const),
                  pl.BlockSpec((tb, d), pidx), srow, row, srow, row, row, row, srow, srow],
        out_specs=(pl.BlockSpec((tb, d), lambda i: (i, 0)), pl.BlockSpec((tb, d), lambda i: (i, 0))),
        compiler_params=_cparams(("parallel",)),
        name="out_proj",
    )(merged, wout, xp, xs, g1p, g1s, n2g, scp, shp, scs, shs)


def _top_rows(x, n):
    row = lax.broadcasted_iota(jnp.int32, x.shape, 0)
    out = []
    for _ in range(n):
        m = jnp.max(x, axis=0, keepdims=True)
        out.append(m)
        first = jnp.min(jnp.where(x == m, row, x.shape[0]), axis=0, keepdims=True)
        x = jnp.where(row == first, NEG_INF, x)
    return out


def _peer_topk_kernel(q_ref, sk_ref, th_ref, ka_ref, s2_ref, e2_ref):
    n = PEER_TOPK + 1
    for h in range(PEER_HEADS):
        s = []
        for p in range(2):
            c0 = (2 * h + p) * PEER_NKEYS
            s.append(lax.dot_general(sk_ref[2 * h + p], q_ref[:, c0:c0 + PEER_NKEYS], NT,
                                     preferred_element_type=F32))
        v1 = _top_rows(s[0], n)
        v2 = _top_rows(s[1], n)
        cands = [v1[a] + v2[b] for a in range(n) for b in range(n) if (a + 1) * (b + 1) <= n]
        cands += [jnp.full_like(v1[0], NEG_INF)] * (-len(cands) % 8)
        sel = _top_rows(jnp.concatenate(cands, axis=0), n)
        zsum = sel[0] * 0.0 + 1.0
        for r in range(1, PEER_TOPK):
            zsum = zsum + jnp.exp(sel[r] - sel[0])
        tau = 0.5 * (sel[PEER_TOPK - 1] + sel[PEER_TOPK])
        th_ref[h] = tau - s[0]
        ka_ref[h] = jnp.exp(s[0] - v1[0]) / zsum
        s2_ref[h] = s[1]
        e2_ref[h] = jnp.exp(s[1] - v2[0])


def _peer_topk(q, sk):
    t = q.shape[0]
    tb = ROW_BLOCK
    shp = jax.ShapeDtypeStruct((PEER_HEADS, PEER_NKEYS, t), F32)
    ospec = pl.BlockSpec((PEER_HEADS, PEER_NKEYS, tb), lambda i: (0, 0, i))
    return pl.pallas_call(
        _peer_topk_kernel,
        out_shape=(shp, shp, shp, shp),
        grid=(t // tb,),
        in_specs=[pl.BlockSpec((tb, q.shape[1]), lambda i: (i, 0)),
                  pl.BlockSpec(sk.shape, lambda i: (0, 0, 0))],
        out_specs=(ospec, ospec, ospec, ospec),
        compiler_params=_cparams(("parallel",)),
        name="peer_topk",
    )(q, sk)


PEER_TOKEN_CHUNK = 256


def _peer_main_kernel(h2t_ref, u_ref, vt_ref, th_ref, ka_ref, s2_ref, e2_ref, o_ref):
    j = pl.program_id(1)
    te = u_ref.shape[0]
    tc = PEER_TOKEN_CHUNK
    n_chunks = h2t_ref.shape[1] // tc

    @pl.when(j == 0)
    def _():
        o_ref[...] = jnp.zeros_like(o_ref)

    def activations(c):
        return jnp.dot(u_ref[...], h2t_ref[:, c * tc:(c + 1) * tc], preferred_element_type=F32)

    def gated(c, at):
        cols = slice(c * tc, (c + 1) * tc)
        ws = []
        for a in range(te // PEER_NKEYS):
            x = at[a * PEER_NKEYS:(a + 1) * PEER_NKEYS]
            ge = 0.5 * x * (1.0 + lax.erf(x * (2.0 ** -0.5)))
            g = jnp.zeros_like(x)
            for h in range(PEER_HEADS):
                g = g + jnp.where(s2_ref[h, :, cols] >= th_ref[h, a:a + 1, cols],
                                  e2_ref[h, :, cols] * ka_ref[h, a:a + 1, cols], 0.0)
            ws.append((g * ge).astype(BF16))
        return jnp.concatenate(ws, axis=0)

    at = activations(0)
    for c in range(n_chunks):
        at_next = activations(c + 1) if c + 1 < n_chunks else None
        w = gated(c, at)
        o_ref[:, c * tc:(c + 1) * tc] += jnp.dot(vt_ref[...], w, preferred_element_type=F32)
        at = at_next


def _peer_main(h2t, u, vt, th, ka, s2, e2):
    d, t = h2t.shape
    n_exp = u.shape[0]
    tm = _pick(t, (768, 512, 256))
    te = 1024
    a_per = te // PEER_NKEYS
    sel_spec = pl.BlockSpec((PEER_HEADS, a_per, tm), lambda i, j: (0, j, i))
    tok_spec = pl.BlockSpec((PEER_HEADS, PEER_NKEYS, tm), lambda i, j: (0, 0, i))
    return pl.pallas_call(
        _peer_main_kernel,
        out_shape=jax.ShapeDtypeStruct((d, t), F32),
        grid=(t // tm, n_exp // te),
        in_specs=[pl.BlockSpec((d, tm), lambda i, j: (0, i)),
                  pl.BlockSpec((te, d), lambda i, j: (j, 0)),
                  pl.BlockSpec((d, te), lambda i, j: (0, j)),
                  sel_spec, sel_spec, tok_spec, tok_spec],
        out_specs=pl.BlockSpec((d, tm), lambda i, j: (0, i)),
        compiler_params=_cparams(("parallel", "arbitrary")),
        name="peer_main",
    )(h2t, u, vt, th, ka, s2, e2)


def _final_kernel(x1_ref, yt_ref, g2p_ref, g2s_ref, op_ref, os_ref, *, n_prompt_blocks):
    i = pl.program_id(0)
    y = yt_ref[...].T

    @pl.when(i < n_prompt_blocks)
    def _():
        op_ref[...] = x1_ref[...] + g2p_ref[...] * y

    @pl.when(i >= n_prompt_blocks)
    def _():
        os_ref[...] = x1_ref[...] + g2s_ref[...] * y


def _final(x1, yt, g2p, g2s, tp):
    t, d = x1.shape
    ts = t - tp
    tb = ROW_BLOCK
    npb, nb = tp // tb, t // tb
    pidx = lambda i: (jnp.minimum(i, npb - 1), 0)
    sidx = lambda i: (jnp.maximum(i - npb, 0), 0)
    return pl.pallas_call(
        functools.partial(_final_kernel, n_prompt_blocks=npb),
        out_shape=(jax.ShapeDtypeStruct((tp, d), F32), jax.ShapeDtypeStruct((ts, d), F32)),
        grid=(nb,),
        in_specs=[pl.BlockSpec((tb, d), lambda i: (i, 0)), pl.BlockSpec((d, tb), lambda i: (0, i)),
                  pl.BlockSpec((1, d), lambda i: (0, 0)), pl.BlockSpec((tb, d), sidx)],
        out_specs=(pl.BlockSpec((tb, d), pidx), pl.BlockSpec((tb, d), sidx)),
        compiler_params=_cparams(("arbitrary",)),
        name="final_residual",
    )(x1, yt, g2p, g2s)


def _layer(layer, xp, xs, c_prompt, c_sample, state_gla, cache_k, cache_v, cache_logf, page_table,
           norm1_g, norm2_g, w_ada, b_ada, w_in, gla_f2, gla_fb, gla_on_g, fox_qn_g, fox_kn_g, fox_fb,
           w_pa, w_pb, w_out, peer_wq, peer_subkeys, peer_u, peer_v):
    bp, tpp, d = xp.shape
    assert bp == 1 and d == D_MODEL
    b, n_new, _ = xs.shape
    tp, ts = bp * tpp, b * n_new
    assert tp % ROW_BLOCK == 0 and ts % ROW_BLOCK == 0
    xp2 = xp.reshape(tp, d)
    xs2 = xs.reshape(ts, d)

    n_c = 1 + b
    n_c_pad = -(-n_c // 8) * 8
    c_all = jnp.concatenate([c_prompt, c_sample, jnp.zeros((n_c_pad - n_c, d), F32)], axis=0)
    mod = _ada(c_all, w_ada, b_ada)
    mod_p = [mod[0:1, i * d:(i + 1) * d] for i in range(6)]
    mod_s = [jnp.repeat(mod[1:n_c, i * d:(i + 1) * d], n_new, axis=0) for i in range(6)]
    sh1p, sc1p, g1p, sh2p, sc2p, g2p = mod_p
    sh1s, sc1s, g1s, sh2s, sc2s, g2s = mod_s

    h = _normmod(xp2, xs2, norm1_g.reshape(1, d), sc1p, sh1p, sc1s, sh1s)

    w1 = jnp.concatenate([w_in[:, :_OFF_GF], w_in[:, _OFF_FQ:_OFF_FK]], axis=1).astype(BF16)
    w2 = w_in[:, _OFF_FK:_OFF_FF].astype(BF16)
    w3 = w_in[:, _OFF_MA:_OFF_END].astype(BF16)
    w4 = jnp.concatenate([w_in[:, _OFF_FF:_OFF_MA], w_in[:, _OFF_GF:_OFF_FQ],
                          jnp.zeros((d, LANES - FOX_HEADS - GLA_GATE_RANK), F32)], axis=1).astype(BF16)
    z1 = _matmul(h, w1, BF16, "in_proj_mixers")
    z2 = _matmul(h, w2, F32, "in_proj_fox_kv")
    z3 = _matmul(h, w3, BF16, "in_proj_gates")
    zs = _matmul(h, w4, F32, "in_proj_small")

    fb_pad = jnp.concatenate([fox_fb, jnp.zeros((LANES - FOX_HEADS,), F32)]).reshape(1, LANES)
    (qn, kb, vb, cum, k_p, v_p, lf_p, k_s, v_s, lf_s) = _fox_prep(
        z1, z2, zs, fox_qn_g.reshape(1, FOX_HD), fox_kn_g.reshape(1, FOX_HD), fb_pad, tp, n_new)
    neg_cum_t = (-LOG2E * cum[:tp, :FOX_HEADS]).T.reshape(FOX_HEADS, 1, tp)
    of_p = _fox_prompt(qn, kb, vb, neg_cum_t, tp)

    qs = qn[tp:].reshape(b, n_new, FOX_HEADS, FOX_HD)
    eye = jnp.eye(FOX_HEADS, dtype=BF16)
    qbd = (qs.transpose(0, 2, 3, 1)[:, :, :, None, :] * eye[None, :, None, :, None])
    qbd = qbd.reshape(b, FOX_W, FOX_HEADS * n_new)
    qbd = jnp.concatenate([qbd, jnp.zeros((b, FOX_W, LANES - FOX_HEADS * n_new), BF16)], axis=2)
    lfn_t = lf_s[:, :FOX_HEADS].reshape(b, n_new, FOX_HEADS).transpose(0, 2, 1)
    lfn_t = jnp.concatenate([lfn_t, jnp.zeros((b, FOX_HEADS, LANES - n_new), F32)], axis=2)
    of_s = _fox_sample(layer, page_table, qbd, k_s.reshape(b, n_new, FOX_W), v_s.reshape(b, n_new, FOX_W), lfn_t,
                       cache_k, cache_v, cache_logf).reshape(ts, FOX_W).astype(BF16)

    f2p = jnp.zeros((LANES, GLA_QK_W), F32).at[FOX_HEADS:FOX_HEADS + GLA_GATE_RANK].set(gla_f2)
    f2p = f2p.reshape(LANES, GLA_HEADS, GLA_DK).transpose(1, 0, 2).astype(BF16)
    fbh = gla_fb.reshape(GLA_HEADS, 1, GLA_DK)
    ong = gla_on_g.reshape(1, GLA_DV)
    og_p, sg_p = _gla_prompt(z1, zs, f2p, fbh, ong, jnp.zeros((GLA_HEADS, GLA_DK, GLA_DV), F32), tp)
    og_s, sg_s = _gla_sample(z1[tp:].astype(F32), zs[tp:], f2p, fbh, ong, state_gla, n_new)

    merged = _merge(og_p, og_s.astype(BF16), of_p, of_s, z3, w_pa.astype(BF16), w_pb.astype(BF16))
    x1, h2 = _outproj(merged, w_out.astype(BF16), xp2, xs2, g1p, g1s, norm2_g.reshape(1, d),
                      sc2p, sh2p, sc2s, sh2s)

    q = _matmul(h2, peer_wq.astype(BF16), BF16, "peer_query")
    sk = peer_subkeys.reshape(PEER_HEADS * 2, PEER_NKEYS, -1).astype(BF16)
    th, ka, s2, e2 = _peer_topk(q, sk)
    yt = _peer_main(h2.T, peer_u.astype(BF16), peer_v.astype(BF16).T, th, ka, s2, e2)
    out_p, out_s = _final(x1, yt, g2p, g2s, tp)

    return (out_p.reshape(bp, tpp, d), out_s.reshape(b, n_new, d),
            sg_p.reshape(bp, GLA_HEADS, GLA_DK, GLA_DV), sg_s,
            k_p.reshape(bp, tpp, FOX_HEADS, FOX_HD), v_p.reshape(bp, tpp, FOX_HEADS, FOX_HD),
            lf_p[:, :FOX_HEADS].reshape(bp, tpp, FOX_HEADS),
            k_s.reshape(b, n_new, FOX_HEADS, FOX_HD), v_s.reshape(b, n_new, FOX_HEADS, FOX_HD),
            lf_s[:, :FOX_HEADS].reshape(b, n_new, FOX_HEADS))


def kernel(x_prompt, x_sample, c_prompt, c_sample, state_gla, cache_k, cache_v, cache_logf, page_table,
           norm1_g, norm2_g, w_ada, b_ada, w_in, gla_f2, gla_fb, gla_on_g, fox_qn_g, fox_kn_g, fox_fb,
           w_pa, w_pb, w_out, peer_wq, peer_subkeys, peer_u, peer_v):
    depth = w_in.shape[0]
    xp, xs = x_prompt, x_sample
    outs = []
    for l in range(depth):
        res = _layer(l, xp, xs, c_prompt, c_sample, state_gla[l], cache_k, cache_v, cache_logf, page_table,
                     norm1_g[l], norm2_g[l], w_ada[l], b_ada[l], w_in[l], gla_f2[l], gla_fb[l], gla_on_g[l],
                     fox_qn_g[l], fox_kn_g[l], fox_fb[l], w_pa[l], w_pb[l], w_out[l],
                     peer_wq[l], peer_subkeys[l], peer_u[l], peer_v[l])
        xp, xs = res[0], res[1]
        outs.append(res[2:])
    stacked = [jnp.stack([o[i] for o in outs]) for i in range(8)]
    return (xp, xs, *stacked)
```

```python
import functools

import jax
import jax.numpy as jnp
from jax import lax
from jax.experimental import pallas as pl
from jax.experimental.pallas import tpu as pltpu

F32 = jnp.float32
BF16 = jnp.bfloat16

D_MODEL = 2048
GLA_HEADS = 4
GLA_DK = 128
GLA_DV = 256
GLA_GATE_RANK = 16
GLA_GATE_TAU = 16.0
GLA_CHUNK = 64
GLA_ROWBLOCK = 16
FOX_HEADS = 8
FOX_HD = 128
FOX_W = FOX_HEADS * FOX_HD
PEER_HEADS = 8
PEER_NKEYS = 128
PEER_TOPK = 16
PAGE_SIZE = 128
EPS = 1e-6

GLA_QK_W = GLA_HEADS * GLA_DK
GLA_V_W = GLA_HEADS * GLA_DV
_OFF_GF = 2 * GLA_QK_W + 2 * GLA_V_W
_OFF_FQ = _OFF_GF + GLA_GATE_RANK
_OFF_FK = _OFF_FQ + FOX_W
_OFF_FV = _OFF_FK + FOX_W
_OFF_FF = _OFF_FV + FOX_W
_OFF_MA = _OFF_FF + FOX_HEADS
_OFF_MB = _OFF_MA + D_MODEL
_OFF_END = _OFF_MB + D_MODEL

LANES = 128
ROW_BLOCK = 256
VMEM_LIMIT = 56 * 1024 * 1024

NT = (((1,), (1,)), ((), ()))
NEG_INF = float("-inf")
LOG2E = 1.4426950408889634


def _cparams(sem):
    return pltpu.CompilerParams(dimension_semantics=sem, vmem_limit_bytes=VMEM_LIMIT)


def _split3(x):
    hi = x.astype(BF16)
    r = x - hi.astype(F32)
    mid = r.astype(BF16)
    lo = (r - mid.astype(F32)).astype(BF16)
    return hi, mid, lo


def _dot01(m01, x):
    acc = None
    for p in _split3(x):
        t = jnp.dot(m01, p, preferred_element_type=F32)
        acc = t if acc is None else acc + t
    return acc


def _dotx01(x, m01):
    acc = None
    for p in _split3(x):
        t = jnp.dot(p, m01, preferred_element_type=F32)
        acc = t if acc is None else acc + t
    return acc


def _log_sigmoid(z):
    return jnp.minimum(z, 0.0) - jnp.log1p(jnp.exp(-jnp.abs(z)))


def _rms(x, g):
    return x * lax.rsqrt(jnp.mean(x * x, axis=-1, keepdims=True) + EPS) * g


def _ada_kernel(c_ref, w_ref, b_ref, o_ref):
    o_ref[...] = jnp.dot(c_ref[...].astype(BF16), w_ref[...].astype(BF16),
                         preferred_element_type=F32) + b_ref[...]


def _ada(c_all, w_ada, b_ada):
    m, k = c_all.shape
    n = w_ada.shape[1]
    tn = 1024
    return pl.pallas_call(
        _ada_kernel,
        out_shape=jax.ShapeDtypeStruct((m, n), F32),
        grid=(n // tn,),
        in_specs=[pl.BlockSpec((m, k), lambda j: (0, 0)),
                  pl.BlockSpec((k, tn), lambda j: (0, j)),
                  pl.BlockSpec((1, tn), lambda j: (0, j))],
        out_specs=pl.BlockSpec((m, tn), lambda j: (0, j)),
        compiler_params=_cparams(("parallel",)),
        name="ada_mod",
    )(c_all, w_ada, b_ada.reshape(1, n))


def _normmod_kernel(xp_ref, xs_ref, g_ref, scp_ref, shp_ref, scs_ref, shs_ref, o_ref, *, n_prompt_blocks):
    i = pl.program_id(0)

    @pl.when(i < n_prompt_blocks)
    def _():
        h = _rms(xp_ref[...], g_ref[...]) * (1.0 + scp_ref[...]) + shp_ref[...]
        o_ref[...] = h.astype(o_ref.dtype)

    @pl.when(i >= n_prompt_blocks)
    def _():
        h = _rms(xs_ref[...], g_ref[...]) * (1.0 + scs_ref[...]) + shs_ref[...]
        o_ref[...] = h.astype(o_ref.dtype)


def _normmod(xp, xs, g, scp, shp, scs, shs):
    tp, d = xp.shape
    ts = xs.shape[0]
    tb = ROW_BLOCK
    npb, nsb = tp // tb, ts // tb
    pidx = lambda i: (jnp.minimum(i, npb - 1), 0)
    sidx = lambda i: (jnp.maximum(i - npb, 0), 0)
    row = pl.BlockSpec((1, d), lambda i: (0, 0))
    return pl.pallas_call(
        functools.partial(_normmod_kernel, n_prompt_blocks=npb),
        out_shape=jax.ShapeDtypeStruct((tp + ts, d), BF16),
        grid=(npb + nsb,),
        in_specs=[pl.BlockSpec((tb, d), pidx), pl.BlockSpec((tb, d), sidx), row, row, row,
                  pl.BlockSpec((tb, d), sidx), pl.BlockSpec((tb, d), sidx)],
        out_specs=pl.BlockSpec((tb, d), lambda i: (i, 0)),
        compiler_params=_cparams(("parallel",)),
        name="norm_mod",
    )(xp, xs, g, scp, shp, scs, shs)


def _mm_kernel(a_ref, w_ref, o_ref):
    o_ref[...] = jnp.dot(a_ref[...], w_ref[...], preferred_element_type=F32).astype(o_ref.dtype)


def _pick(n, cands):
    for c in cands:
        if n % c == 0:
            return c
    return n


def _matmul(a, w, out_dtype, name):
    m, k = a.shape
    n = w.shape[1]
    tm = _pick(m, (1024, 768, 512, 256))
    tn = _pick(n, (1024, 512, 256, 128))
    return pl.pallas_call(
        _mm_kernel,
        out_shape=jax.ShapeDtypeStruct((m, n), out_dtype),
        grid=(m // tm, n // tn),
        in_specs=[pl.BlockSpec((tm, k), lambda i, j: (i, 0)),
                  pl.BlockSpec((k, tn), lambda i, j: (0, j))],
        out_specs=pl.BlockSpec((tm, tn), lambda i, j: (i, j)),
        compiler_params=_cparams(("parallel", "parallel")),
        name=name,
    )(a, w)


def _fox_prep_kernel(fq_ref, fkv_ref, zs_ref, qg_ref, kg_ref, fb_ref,
                     qn_ref, kb_ref, vb_ref, cum_ref, kp_ref, vp_ref, lfp_ref, ks_ref, vs_ref, lfs_ref,
                     carry_ref, *, n_prompt_blocks, seg_len):
    i = pl.program_id(0)
    tb = fq_ref.shape[0]
    is_prompt = i < n_prompt_blocks

    @pl.when(i == 0)
    def _():
        carry_ref[...] = jnp.zeros_like(carry_ref)

    scale = FOX_HD ** -0.5 * jnp.where(is_prompt, LOG2E, 1.0).astype(F32)
    fq = fq_ref[...].astype(F32)
    fkv = fkv_ref[...]
    kn = []
    for h in range(FOX_HEADS):
        sl = slice(h * FOX_HD, (h + 1) * FOX_HD)
        qn_ref[:, sl] = (_rms(fq[:, sl], qg_ref[...]) * scale).astype(qn_ref.dtype)
        kn.append(_rms(fkv[:, sl], kg_ref[...]))
    kn = jnp.concatenate(kn, axis=1)
    v = fkv[:, FOX_W:]
    kb_ref[...] = kn.astype(kb_ref.dtype)
    vb_ref[...] = v.astype(vb_ref.dtype)

    lf = _log_sigmoid(zs_ref[...] + fb_ref[...])
    row = lax.broadcasted_iota(jnp.int32, (tb, tb), 0)
    col = lax.broadcasted_iota(jnp.int32, (tb, tb), 1)

    @pl.when(is_prompt)
    def _():
        tri = jnp.where(col <= row, 1.0, 0.0).astype(BF16)
        cum = _dot01(tri, lf) + carry_ref[...]
        carry_ref[...] = cum[tb - 1:tb, :]
        cum_ref[...] = cum
        kp_ref[...] = kn
        vp_ref[...] = v
        lfp_ref[...] = lf

    @pl.when(jnp.logical_not(is_prompt))
    def _():
        tri = jnp.where(col <= row, jnp.where((row // seg_len) == (col // seg_len), 1.0, 0.0), 0.0).astype(BF16)
        cum_ref[...] = _dot01(tri, lf)
        ks_ref[...] = kn
        vs_ref[...] = v
        lfs_ref[...] = lf


def _fox_prep(z1, z2, zs, qg, kg, fb_pad, tp, seg_len):
    t_all = z1.shape[0]
    ts = t_all - tp
    tb = ROW_BLOCK
    npb, nb = tp // tb, t_all // tb
    pidx = lambda i: (jnp.minimum(i, npb - 1), 0)
    sidx = lambda i: (jnp.maximum(i - npb, 0), 0)
    allidx = lambda i: (i, 0)
    row = pl.BlockSpec((1, LANES), lambda i: (0, 0))
    fq_block = (_OFF_GF) // FOX_W
    return pl.pallas_call(
        functools.partial(_fox_prep_kernel, n_prompt_blocks=npb, seg_len=seg_len),
        out_shape=(jax.ShapeDtypeStruct((t_all, FOX_W), BF16),
                   jax.ShapeDtypeStruct((t_all, FOX_W), BF16),
                   jax.ShapeDtypeStruct((t_all, FOX_W), BF16),
                   jax.ShapeDtypeStruct((t_all, LANES), F32),
                   jax.ShapeDtypeStruct((tp, FOX_W), F32),
                   jax.ShapeDtypeStruct((tp, FOX_W), F32),
                   jax.ShapeDtypeStruct((tp, LANES), F32),
                   jax.ShapeDtypeStruct((ts, FOX_W), F32),
                   jax.ShapeDtypeStruct((ts, FOX_W), F32),
                   jax.ShapeDtypeStruct((ts, LANES), F32)),
        grid=(nb,),
        in_specs=[pl.BlockSpec((tb, FOX_W), lambda i: (i, fq_block)),
                  pl.BlockSpec((tb, 2 * FOX_W), allidx),
                  pl.BlockSpec((tb, LANES), allidx), row, row, row],
        out_specs=(pl.BlockSpec((tb, FOX_W), allidx), pl.BlockSpec((tb, FOX_W), allidx),
                   pl.BlockSpec((tb, FOX_W), allidx), pl.BlockSpec((tb, LANES), allidx),
                   pl.BlockSpec((tb, FOX_W), pidx), pl.BlockSpec((tb, FOX_W), pidx),
                   pl.BlockSpec((tb, LANES), pidx),
                   pl.BlockSpec((tb, FOX_W), sidx), pl.BlockSpec((tb, FOX_W), sidx),
                   pl.BlockSpec((tb, LANES), sidx)),
        scratch_shapes=[pltpu.VMEM((1, LANES), F32)],
        compiler_params=_cparams(("arbitrary",)),
        name="fox_prep",
    )(z1, z2, zs, qg, kg, fb_pad)


FOX_HEADS_PER_STEP = 4


def _fox_prompt_kernel(q_ref, k_ref, v_ref, nc_ref, o_ref, *, tk):
    qi = pl.program_id(1)
    tq = q_ref.shape[0]
    nh = q_ref.shape[1] // FOX_HD
    n_full = (qi * tq) // tk
    n_diag = -(-tq // tk)

    def block(c0, masked, carry):
        out = []
        for h in range(nh):
            m, l, acc = carry[h]
            sl = slice(h * FOX_HD, (h + 1) * FOX_HD)
            s = lax.dot_general(q_ref[:, sl], k_ref[pl.ds(c0, tk), sl], NT, preferred_element_type=F32)
            s = s + nc_ref[h, :, pl.ds(c0, tk)]
            if masked:
                row = qi * tq + lax.broadcasted_iota(jnp.int32, (tq, tk), 0)
                col = c0 + lax.broadcasted_iota(jnp.int32, (tq, tk), 1)
                s = jnp.where(col <= row, s, NEG_INF)
            m_new = jnp.maximum(m, jnp.max(s, axis=1, keepdims=True))
            alpha = jnp.exp2(m - m_new)
            p = jnp.exp2(s - m_new)
            l = alpha * l + jnp.sum(p, axis=1, keepdims=True)
            acc = alpha * acc + jnp.dot(p.astype(BF16), v_ref[pl.ds(c0, tk), sl], preferred_element_type=F32)
            out.append((m_new, l, acc))
        return tuple(out)

    init = tuple((jnp.full((tq, 1), NEG_INF, F32), jnp.zeros((tq, 1), F32), jnp.zeros((tq, FOX_HD), F32))
                 for _ in range(nh))
    carry = lax.fori_loop(0, n_full, lambda i, c: block(pl.multiple_of(i * tk, tk), False, c), init)
    final = lax.fori_loop(n_full, n_full + n_diag, lambda i, c: block(pl.multiple_of(i * tk, tk), True, c), carry)
    for h in range(nh):
        _, l, acc = final[h]
        o_ref[:, h * FOX_HD:(h + 1) * FOX_HD] = (acc / l).astype(o_ref.dtype)


def _fox_prompt(qn, kb, vb, neg_cum_t, tp):
    tk = _pick(tp, (512, 256, 128))
    tq = tk
    nh = FOX_HEADS_PER_STEP
    w = nh * FOX_HD
    return pl.pallas_call(
        functools.partial(_fox_prompt_kernel, tk=tk),
        out_shape=jax.ShapeDtypeStruct((tp, FOX_W), BF16),
        grid=(FOX_HEADS // nh, tp // tq),
        in_specs=[pl.BlockSpec((tq, w), lambda g, i: (i, g)),
                  pl.BlockSpec((tp, w), lambda g, i: (0, g)),
                  pl.BlockSpec((tp, w), lambda g, i: (0, g)),
                  pl.BlockSpec((nh, 1, tp), lambda g, i: (g, 0, 0))],
        out_specs=pl.BlockSpec((tq, w), lambda g, i: (i, g)),
        compiler_params=_cparams(("parallel", "parallel")),
        name="fox_prompt",
    )(qn, kb, vb, neg_cum_t)


def _expand_heads(x_t, n_q):
    hi, mid, lo = _split3(x_t)
    n = x_t.shape[0]
    stacked = jnp.concatenate([hi.astype(F32), mid.astype(F32), lo.astype(F32),
                               jnp.zeros((LANES - 3 * n, LANES), F32)], axis=0)
    k = lax.broadcasted_iota(jnp.int32, (LANES, LANES), 0)
    c = lax.broadcasted_iota(jnp.int32, (LANES, LANES), 1)
    e3 = jnp.where(k < 3 * n, jnp.where(c < n * n_q, jnp.where((k % n) == (c // n_q), 1.0, 0.0), 0.0), 0.0)
    return jnp.dot(stacked.T.astype(BF16), e3.astype(BF16), preferred_element_type=F32)


def _page_to_rows(page_ref, buf_ref, p):
    for h in range(FOX_HEADS):
        x = page_ref[pl.ds(h, PAGE_SIZE, stride=FOX_HEADS), :]
        buf_ref[p * PAGE_SIZE:(p + 1) * PAGE_SIZE, h * FOX_HD:(h + 1) * FOX_HD] = x.astype(BF16)


def _fox_sample_kernel(pt_ref, qbd_ref, kn_ref, vn_ref, lfn_ref, *rest, pages_per_step, n_groups):
    pg = pages_per_step
    k_refs = rest[:pg]
    lf_refs = rest[pg:2 * pg]
    v_refs = rest[2 * pg:3 * pg]
    o_ref = rest[3 * pg]
    s_all, s_new, m_ref, l_ref, acc_ref, carry_ref, kv_buf, eall_ref = rest[3 * pg + 1:]
    j = pl.program_id(1)
    n_new = kn_ref.shape[0]
    hp = FOX_HEADS
    rows = pg * PAGE_SIZE
    ki = lax.broadcasted_iota(jnp.int32, (LANES, LANES), 0)
    ci = lax.broadcasted_iota(jnp.int32, (LANES, LANES), 1)

    @pl.when(j == 0)
    def _():
        pad = jnp.zeros((LANES - n_new, FOX_W), F32)
        knp = jnp.concatenate([kn_ref[...], pad], axis=0).astype(BF16)
        incl = jnp.where(ki <= ci, 1.0, 0.0).astype(BF16)
        cn_t = _dotx01(lfn_ref[...], incl)
        s = jnp.dot(knp, qbd_ref[...], preferred_element_type=F32) - _expand_heads(cn_t, n_new)
        s = jnp.where(ki <= ci % n_new, s, NEG_INF)
        s = jnp.where(ki < n_new, s, NEG_INF)
        s = jnp.where(ci < FOX_HEADS * n_new, s, NEG_INF)
        s_new[...] = s
        m_ref[...] = jnp.max(s, axis=0, keepdims=True)
        carry_ref[...] = jnp.zeros_like(carry_ref)
        kk = lax.broadcasted_iota(jnp.int32, eall_ref.shape, 0)
        cc = lax.broadcasted_iota(jnp.int32, eall_ref.shape, 1)
        hit = kk == (cc // LANES) * hp + (cc % LANES) // n_new
        eall_ref[...] = jnp.where(hit, jnp.where(cc % LANES < FOX_HEADS * n_new, 1.0, 0.0), 0.0).astype(BF16)

    @pl.when(j < n_groups)
    def _():
        g = n_groups - 1 - j
        for p in range(pg):
            _page_to_rows(k_refs[p], kv_buf, p)
        s = jnp.dot(kv_buf[...], qbd_ref[...], preferred_element_type=F32)
        tiles = [lf_refs[p][...] for p in range(pg)]
        if pg * hp < LANES:
            tiles.append(jnp.zeros((LANES - pg * hp, PAGE_SIZE), F32))
        x = jnp.concatenate(tiles, axis=0).T
        later_t = jnp.where(ci > ki, 1.0, 0.0).astype(BF16)
        same_head = (ki % hp) == (ci % hp)
        later_page = jnp.where(same_head, jnp.where(ki // hp > ci // hp, 1.0, 0.0), 0.0).astype(BF16)
        any_page = jnp.where(same_head, 1.0, 0.0).astype(BF16)
        tot = jnp.broadcast_to(jnp.sum(x, axis=0, keepdims=True), (8, LANES))
        tail = _dot01(later_t, x) + (_dotx01(tot, later_page)[0:1] + carry_ref[...])
        carry_ref[...] = carry_ref[...] + _dotx01(tot, any_page)[0:1]
        ex = jnp.dot(jnp.concatenate(list(_split3(tail)), axis=0), eall_ref[...], preferred_element_type=F32)
        blocks = []
        for p in range(pg):
            cols = slice(p * LANES, (p + 1) * LANES)
            add = ex[0:LANES, cols] + ex[LANES:2 * LANES, cols] + ex[2 * LANES:3 * LANES, cols]
            blocks.append(s[p * PAGE_SIZE:(p + 1) * PAGE_SIZE] + add)
        s = jnp.concatenate(blocks, axis=0)
        s_all[pl.ds(pl.multiple_of(g * rows, rows), rows), :] = s
        m_ref[...] = jnp.maximum(m_ref[...], jnp.max(s, axis=0, keepdims=True))

    @pl.when(j == n_groups)
    def _():
        pad = jnp.zeros((LANES - n_new, FOX_W), F32)
        vnp = jnp.concatenate([vn_ref[...], pad], axis=0).astype(BF16)
        p = jnp.exp(s_new[...] - m_ref[...])
        l_ref[...] = jnp.sum(p, axis=0, keepdims=True)
        acc_ref[...] = jnp.dot(p.T.astype(BF16), vnp, preferred_element_type=F32)

    @pl.when(j >= n_groups)
    def _():
        g = 2 * n_groups - 1 - j
        for p in range(pg):
            _page_to_rows(v_refs[p], kv_buf, p)
        pexp = jnp.exp(s_all[pl.ds(pl.multiple_of(g * rows, rows), rows), :] - m_ref[...])
        l_ref[...] += jnp.sum(pexp, axis=0, keepdims=True)
        acc_ref[...] += jnp.dot(pexp.T.astype(BF16), kv_buf[...], preferred_element_type=F32)

    @pl.when(j == 2 * n_groups - 1)
    def _():
        l_col = jnp.broadcast_to(l_ref[...], (LANES, LANES)).T
        for h in range(FOX_HEADS):
            rows = slice(h * n_new, (h + 1) * n_new)
            cols = slice(h * FOX_HD, (h + 1) * FOX_HD)
            o_ref[:, cols] = acc_ref[rows, cols] / l_col[rows, :]


def _fox_sample(layer, page_table, qbd, kn, vn, lfn_t, ck, cv, clf):
    ck = ck.reshape(ck.shape[0], ck.shape[1], PAGE_SIZE * FOX_HEADS, FOX_HD)
    cv = cv.reshape(cv.shape[0], cv.shape[1], PAGE_SIZE * FOX_HEADS, FOX_HD)
    clf_t = clf.transpose(0, 1, 3, 2)
    b, n_pages = page_table.shape
    n_new = kn.shape[1]
    pg = _pick(n_pages, (16, 8, 4, 2))
    ng = n_pages // pg
    assert FOX_HEADS * pg <= LANES and FOX_HEADS * n_new <= LANES

    def page_map(r, phase_v, n_trailing):
        def index(bi, j, pt):
            step = jnp.maximum(j - ng, 0) if phase_v else jnp.minimum(j, ng - 1)
            return (layer, pt[bi, (ng - 1 - step) * pg + r]) + (0,) * n_trailing
        return index

    per_b = lambda bi, j, pt: (bi, 0, 0)
    page_blk = (None, None, PAGE_SIZE * FOX_HEADS, FOX_HD)
    in_specs = [pl.BlockSpec((None, FOX_W, LANES), per_b),
                pl.BlockSpec((None, n_new, FOX_W), per_b),
                pl.BlockSpec((None, n_new, FOX_W), per_b),
                pl.BlockSpec((None, FOX_HEADS, LANES), per_b)]
    in_specs += [pl.BlockSpec(page_blk, page_map(r, False, 2)) for r in range(pg)]
    in_specs += [pl.BlockSpec((None, None, FOX_HEADS, PAGE_SIZE), page_map(r, False, 2)) for r in range(pg)]
    in_specs += [pl.BlockSpec(page_blk, page_map(r, True, 2)) for r in range(pg)]
    grid_spec = pltpu.PrefetchScalarGridSpec(
        num_scalar_prefetch=1,
        grid=(b, 2 * ng),
        in_specs=in_specs,
        out_specs=pl.BlockSpec((None, n_new, FOX_W), per_b),
        scratch_shapes=[pltpu.VMEM((n_pages * PAGE_SIZE, LANES), F32),
                        pltpu.VMEM((LANES, LANES), F32),
                        pltpu.VMEM((1, LANES), F32),
                        pltpu.VMEM((1, LANES), F32),
                        pltpu.VMEM((LANES, FOX_W), F32),
                        pltpu.VMEM((1, LANES), F32),
                        pltpu.VMEM((pg * PAGE_SIZE, FOX_W), BF16),
                        pltpu.VMEM((LANES, pg * LANES), BF16)],
    )
    return pl.pallas_call(
        functools.partial(_fox_sample_kernel, pages_per_step=pg, n_groups=ng),
        out_shape=jax.ShapeDtypeStruct((b, n_new, FOX_W), F32),
        grid_spec=grid_spec,
        compiler_params=_cparams(("parallel", "arbitrary")),
        name="fox_sample",
    )(page_table, qbd, kn, vn, lfn_t, *([ck] * pg), *([clf_t] * pg), *([cv] * pg))


def _gla_chunk(q, k, v, zs, f2p, fb, s_state, n_row_blocks, rb):
    c = q.shape[0]
    z = jnp.dot(zs.astype(BF16), f2p, preferred_element_type=F32) + fb
    la = _log_sigmoid(z) * (1.0 / GLA_GATE_TAU)
    ri = lax.broadcasted_iota(jnp.int32, (c, c), 0)
    ci = lax.broadcasted_iota(jnp.int32, (c, c), 1)
    b = _dot01(jnp.where(ci <= ri, 1.0, 0.0).astype(BF16), la)
    bl = b[c - 1:c]
    rbi = lax.broadcasted_iota(jnp.int32, (rb, c), 0)
    cbi = lax.broadcasted_iota(jnp.int32, (rb, c), 1)
    rows = []
    for i in range(n_row_blocks):
        r0 = i * rb
        qb, kb_, bb = q[r0:r0 + rb], k[r0:r0 + rb], b[r0:r0 + rb]
        a = jnp.zeros((rb, c), F32)
        if i > 0:
            bref = b[r0:r0 + 1]
            qt = qb * jnp.exp(bb - bref)
            kt = k * jnp.exp(jnp.minimum(bref - b, 0.0))
            a = lax.dot_general(qt.astype(BF16), kt.astype(BF16), NT, preferred_element_type=F32)
            a = jnp.where(cbi < r0, a, 0.0)
        for s in range(rb):
            e = jnp.exp(jnp.minimum(bb - bb[s:s + 1], 0.0))
            colv = jnp.sum(qb * kb_[s:s + 1] * e, axis=-1, keepdims=True)
            a = jnp.where(cbi == r0 + s, jnp.where(rbi >= s, colv, 0.0), a)
        rows.append(a)
    a = rows[0] if len(rows) == 1 else jnp.concatenate(rows, axis=0)
    nr = n_row_blocks * rb
    o = (jnp.dot(a.astype(BF16), v, preferred_element_type=F32)
         + jnp.dot((q[:nr] * jnp.exp(b[:nr])).astype(BF16), s_state.astype(BF16), preferred_element_type=F32))
    kh = (k * jnp.exp(bl - b)).astype(BF16)
    blc = jnp.exp(jnp.broadcast_to(bl, (GLA_DK, GLA_DK)).T)
    upd = lax.dot_general(kh, v, (((0,), (0,)), ((), ())), preferred_element_type=F32)
    s_next = jnp.concatenate([blc] * (GLA_DV // GLA_DK), axis=1) * s_state + upd
    return o, s_next


def _gla_out(o, gr, ong):
    return _rms(o, ong) * (gr * jax.nn.sigmoid(gr))


def _gla_prompt_kernel(q_ref, k_ref, v_ref, gr_ref, zs_ref, f2p_ref, fb_ref, ong_ref, s0_ref,
                       og_ref, sout_ref, s_ref):
    i = pl.program_id(0)
    tb = q_ref.shape[0]
    c = GLA_CHUNK

    @pl.when(i == 0)
    def _():
        s_ref[...] = s0_ref[...]

    def chunk(n, carry):
        r = pl.multiple_of(n * c, c)
        zs = zs_ref[pl.ds(r, c), :]
        for h in range(GLA_HEADS):
            qk = slice(h * GLA_DK, (h + 1) * GLA_DK)
            vv = slice(h * GLA_DV, (h + 1) * GLA_DV)
            q = q_ref[pl.ds(r, c), qk].astype(F32) * (GLA_DK ** -0.5)
            k = k_ref[pl.ds(r, c), qk].astype(F32)
            o, s_next = _gla_chunk(q, k, v_ref[pl.ds(r, c), vv], zs, f2p_ref[h], fb_ref[h],
                                   s_ref[h], c // GLA_ROWBLOCK, GLA_ROWBLOCK)
            s_ref[h] = s_next
            og_ref[pl.ds(r, c), vv] = _gla_out(o, gr_ref[pl.ds(r, c), vv].astype(F32),
                                               ong_ref[...]).astype(og_ref.dtype)
        return carry

    lax.fori_loop(0, tb // c, chunk, 0)

    @pl.when(i == pl.num_programs(0) - 1)
    def _():
        sout_ref[...] = s_ref[...]


def _gla_prompt(z1, zs, f2p, fb, ong, s0, tp):
    tb = _pick(tp, (512, 256, 128, 64))
    whole = lambda i: (0, 0, 0)
    return pl.pallas_call(
        _gla_prompt_kernel,
        out_shape=(jax.ShapeDtypeStruct((tp, GLA_V_W), BF16),
                   jax.ShapeDtypeStruct((GLA_HEADS, GLA_DK, GLA_DV), F32)),
        grid=(tp // tb,),
        in_specs=[pl.BlockSpec((tb, GLA_QK_W), lambda i: (i, 0)),
                  pl.BlockSpec((tb, GLA_QK_W), lambda i: (i, 1)),
                  pl.BlockSpec((tb, GLA_V_W), lambda i: (i, 1)),
                  pl.BlockSpec((tb, GLA_V_W), lambda i: (i, 2)),
                  pl.BlockSpec((tb, LANES), lambda i: (i, 0)),
                  pl.BlockSpec((GLA_HEADS, LANES, GLA_DK), whole),
                  pl.BlockSpec((GLA_HEADS, 1, GLA_DK), whole),
                  pl.BlockSpec((1, GLA_DV), lambda i: (0, 0)),
                  pl.BlockSpec((GLA_HEADS, GLA_DK, GLA_DV), whole)],
        out_specs=(pl.BlockSpec((tb, GLA_V_W), lambda i: (i, 0)),
                   pl.BlockSpec((GLA_HEADS, GLA_DK, GLA_DV), whole)),
        scratch_shapes=[pltpu.VMEM((GLA_HEADS, GLA_DK, GLA_DV), F32)],
        compiler_params=_cparams(("arbitrary",)),
        name="gla_prompt",
    )(z1, z1, z1, z1, zs, f2p, fb, ong, s0)


def _gla_sample_kernel(q_ref, k_ref, v_ref, gr_ref, zs_ref, f2p_ref, fb_ref, ong_ref, s0_ref, og_ref, sout_ref):
    t = q_ref.shape[0]
    pad = lambda x: jnp.concatenate([x, jnp.zeros((LANES - t, x.shape[1]), x.dtype)], axis=0)
    zs = pad(zs_ref[...])
    rowi = lax.broadcasted_iota(jnp.int32, (LANES, GLA_DK), 0)
    for h in range(GLA_HEADS):
        qk = slice(h * GLA_DK, (h + 1) * GLA_DK)
        vv = slice(h * GLA_DV, (h + 1) * GLA_DV)
        q = pad(q_ref[:, qk] * (GLA_DK ** -0.5))
        k = pad(k_ref[:, qk])
        v = pad(v_ref[:, vv]).astype(BF16)
        fb = jnp.where(rowi < t, fb_ref[h], 1e4)
        o, s_next = _gla_chunk(q, k, v, zs, f2p_ref[h], fb, s0_ref[h], 1, t)
        og_ref[:, vv] = _gla_out(o, gr_ref[:, vv], ong_ref[...])
        sout_ref[h] = s_next


def _gla_sample(z1s, zss, f2p, fb, ong, s0, n_new):
    ts = z1s.shape[0]
    b = ts // n_new
    whole = lambda bi: (0, 0, 0)
    state_spec = pl.BlockSpec((None, GLA_HEADS, GLA_DK, GLA_DV), lambda bi: (bi, 0, 0, 0))
    return pl.pallas_call(
        _gla_sample_kernel,
        out_shape=(jax.ShapeDtypeStruct((ts, GLA_V_W), F32),
                   jax.ShapeDtypeStruct((b, GLA_HEADS, GLA_DK, GLA_DV), F32)),
        grid=(b,),
        in_specs=[pl.BlockSpec((n_new, GLA_QK_W), lambda bi: (bi, 0)),
                  pl.BlockSpec((n_new, GLA_QK_W), lambda bi: (bi, 1)),
                  pl.BlockSpec((n_new, GLA_V_W), lambda bi: (bi, 1)),
                  pl.BlockSpec((n_new, GLA_V_W), lambda bi: (bi, 2)),
                  pl.BlockSpec((n_new, LANES), lambda bi: (bi, 0)),
                  pl.BlockSpec((GLA_HEADS, LANES, GLA_DK), whole),
                  pl.BlockSpec((GLA_HEADS, 1, GLA_DK), whole),
                  pl.BlockSpec((1, GLA_DV), lambda bi: (0, 0)),
                  state_spec],
        out_specs=(pl.BlockSpec((n_new, GLA_V_W), lambda bi: (bi, 0)), state_spec),
        compiler_params=_cparams(("parallel",)),
        name="gla_sample",
    )(z1s, z1s, z1s, z1s, zss, f2p, fb, ong, s0)


def _merge_kernel(ogp_ref, ogs_ref, ofp_ref, ofs_ref, gate_ref, wpa_ref, wpb_ref, o_ref, *, n_prompt_blocks):
    i = pl.program_id(0)

    def run(og, of):
        d = o_ref.shape[1]
        ma = gate_ref[:, :d].astype(F32)
        mb = gate_ref[:, d:].astype(F32)
        a = jnp.dot(og, wpa_ref[...], preferred_element_type=F32)
        b = jnp.dot(of, wpb_ref[...], preferred_element_type=F32)
        o_ref[...] = (jax.nn.sigmoid(ma) * a + jax.nn.sigmoid(mb) * b).astype(o_ref.dtype)

    @pl.when(i < n_prompt_blocks)
    def _():
        run(ogp_ref[...], ofp_ref[...])

    @pl.when(i >= n_prompt_blocks)
    def _():
        run(ogs_ref[...], ofs_ref[...])


def _merge(ogp, ogs, ofp, ofs, z3, wpa, wpb):
    tp, ts = ogp.shape[0], ogs.shape[0]
    d = wpa.shape[1]
    tb = ROW_BLOCK
    npb, nb = tp // tb, (tp + ts) // tb
    pidx = lambda i: (jnp.minimum(i, npb - 1), 0)
    sidx = lambda i: (jnp.maximum(i - npb, 0), 0)
    const = lambda i: (0, 0)
    return pl.pallas_call(
        functools.partial(_merge_kernel, n_prompt_blocks=npb),
        out_shape=jax.ShapeDtypeStruct((tp + ts, d), BF16),
        grid=(nb,),
        in_specs=[pl.BlockSpec((tb, GLA_V_W), pidx), pl.BlockSpec((tb, GLA_V_W), sidx),
                  pl.BlockSpec((tb, FOX_W), pidx), pl.BlockSpec((tb, FOX_W), sidx),
                  pl.BlockSpec((tb, 2 * d), lambda i: (i, 0)),
                  pl.BlockSpec(wpa.shape, const), pl.BlockSpec(wpb.shape, const)],
        out_specs=pl.BlockSpec((tb, d), lambda i: (i, 0)),
        compiler_params=_cparams(("parallel",)),
        name="merge",
    )(ogp, ogs, ofp, ofs, z3, wpa, wpb)


def _outproj_kernel(m_ref, w_ref, xp_ref, xs_ref, g1p_ref, g1s_ref, n2g_ref, scp_ref, shp_ref, scs_ref, shs_ref,
                    x1_ref, h2_ref, *, n_prompt_blocks):
    i = pl.program_id(0)
    y = jnp.dot(m_ref[...], w_ref[...], preferred_element_type=F32)

    def run(x, g1, sc, sh):
        x1 = x + g1 * y
        x1_ref[...] = x1
        h2_ref[...] = (_rms(x1, n2g_ref[...]) * (1.0 + sc) + sh).astype(h2_ref.dtype)

    @pl.when(i < n_prompt_blocks)
    def _():
        run(xp_ref[...], g1p_ref[...], scp_ref[...], shp_ref[...])

    @pl.when(i >= n_prompt_blocks)
    def _():
        run(xs_ref[...], g1s_ref[...], scs_ref[...], shs_ref[...])


def _outproj(merged, wout, xp, xs, g1p, g1s, n2g, scp, shp, scs, shs):
    tp, d = xp.shape
    ts = xs.shape[0]
    tb = ROW_BLOCK
    npb, nb = tp // tb, (tp + ts) // tb
    pidx = lambda i: (jnp.minimum(i, npb - 1), 0)
    sidx = lambda i: (jnp.maximum(i - npb, 0), 0)
    const = lambda i: (0, 0)
    row = pl.BlockSpec((1, d), const)
    srow = pl.BlockSpec((tb, d), sidx)
    return pl.pallas_call(
        functools.partial(_outproj_kernel, n_prompt_blocks=npb),
        out_shape=(jax.ShapeDtypeStruct((tp + ts, d), F32), jax.ShapeDtypeStruct((tp + ts, d), BF16)),
        grid=(nb,),
        in_specs=[pl.BlockSpec((tb, d), lambda i: (i, 0)), pl.BlockSpec(wout.shape, const),
                  pl.BlockSpec((tb, d), pidx), srow, row, srow, row, row, row, srow, srow],
        out_specs=(pl.BlockSpec((tb, d), lambda i: (i, 0)), pl.BlockSpec((tb, d), lambda i: (i, 0))),
        compiler_params=_cparams(("parallel",)),
        name="out_proj",
    )(merged, wout, xp, xs, g1p, g1s, n2g, scp, shp, scs, shs)


def _top_rows(x, n):
    row = lax.broadcasted_iota(jnp.int32, x.shape, 0)
    out = []
    for _ in range(n):
        m = jnp.max(x, axis=0, keepdims=True)
        out.append(m)
        first = jnp.min(jnp.where(x == m, row, x.shape[0]), axis=0, keepdims=True)
        x = jnp.where(row == first, NEG_INF, x)
    return out


def _top_rows_distinct(x, n):
    out = []
    for _ in range(n):
        m = jnp.max(x, axis=0, keepdims=True)
        out.append(m)
        x = jnp.where(x == m, NEG_INF, x)
    used = jnp.sum(jnp.where(x == NEG_INF, 1.0, 0.0), axis=0, keepdims=True)
    return out, used


def _peer_topk_kernel(q_ref, sk_ref, th_ref, ka_ref, s2_ref, e2_ref):
    n = PEER_TOPK + 1
    scores = []
    for hp in range(2 * PEER_HEADS):
        c0 = hp * PEER_NKEYS
        scores.append(lax.dot_general(sk_ref[hp], q_ref[:, c0:c0 + PEER_NKEYS], NT,
                                      preferred_element_type=F32))
    fast = [_top_rows_distinct(s, n) for s in scores]
    most_used = jnp.max(jnp.concatenate([used for _, used in fast], axis=0))
    tops = lax.cond(most_used <= n,
                    lambda: [vals for vals, _ in fast],
                    lambda: [_top_rows(s, n) for s in scores])
    for h in range(PEER_HEADS):
        s = scores[2 * h:2 * h + 2]
        v1, v2 = tops[2 * h], tops[2 * h + 1]
        cands = [v1[a] + v2[b] for a in range(n) for b in range(n) if (a + 1) * (b + 1) <= n]
        cands += [jnp.full_like(v1[0], NEG_INF)] * (-len(cands) % 8)
        sel = _top_rows(jnp.concatenate(cands, axis=0), n)
        zsum = sel[0] * 0.0 + 1.0
        for r in range(1, PEER_TOPK):
            zsum = zsum + jnp.exp(sel[r] - sel[0])
        tau = 0.5 * (sel[PEER_TOPK - 1] + sel[PEER_TOPK])
        th_ref[h] = tau - s[0]
        ka_ref[h] = jnp.exp(s[0] - v1[0]) / zsum
        s2_ref[h] = s[1]
        e2_ref[h] = jnp.exp(s[1] - v2[0])


def _peer_topk(q, sk):
    t = q.shape[0]
    tb = ROW_BLOCK
    shp = jax.ShapeDtypeStruct((PEER_HEADS, PEER_NKEYS, t), F32)
    ospec = pl.BlockSpec((PEER_HEADS, PEER_NKEYS, tb), lambda i: (0, 0, i))
    return pl.pallas_call(
        _peer_topk_kernel,
        out_shape=(shp, shp, shp, shp),
        grid=(t // tb,),
        in_specs=[pl.BlockSpec((tb, q.shape[1]), lambda i: (i, 0)),
                  pl.BlockSpec(sk.shape, lambda i: (0, 0, 0))],
        out_specs=(ospec, ospec, ospec, ospec),
        compiler_params=_cparams(("parallel",)),
        name="peer_topk",
    )(q, sk)


PEER_TOKEN_CHUNK = 256


def _peer_main_kernel(h2t_ref, u_ref, vt_ref, th_ref, ka_ref, s2_ref, e2_ref, o_ref):
    j = pl.program_id(1)
    te = u_ref.shape[0]
    tc = PEER_TOKEN_CHUNK
    n_chunks = h2t_ref.shape[1] // tc

    @pl.when(j == 0)
    def _():
        o_ref[...] = jnp.zeros_like(o_ref)

    def activations(c):
        return jnp.dot(u_ref[...], h2t_ref[:, c * tc:(c + 1) * tc], preferred_element_type=F32)

    def gated(c, at):
        cols = slice(c * tc, (c + 1) * tc)
        ws = []
        for a in range(te // PEER_NKEYS):
            x = at[a * PEER_NKEYS:(a + 1) * PEER_NKEYS]
            ge = 0.5 * x * (1.0 + lax.erf(x * (2.0 ** -0.5)))
            g = jnp.zeros_like(x)
            for h in range(PEER_HEADS):
                g = g + jnp.where(s2_ref[h, :, cols] >= th_ref[h, a:a + 1, cols],
                                  e2_ref[h, :, cols] * ka_ref[h, a:a + 1, cols], 0.0)
            ws.append((g * ge).astype(BF16))
        return jnp.concatenate(ws, axis=0)

    at = activations(0)
    for c in range(n_chunks):
        at_next = activations(c + 1) if c + 1 < n_chunks else None
        w = gated(c, at)
        o_ref[:, c * tc:(c + 1) * tc] += jnp.dot(vt_ref[...], w, preferred_element_type=F32)
        at = at_next


def _peer_main(h2t, u, vt, th, ka, s2, e2):
    d, t = h2t.shape
    n_exp = u.shape[0]
    tm = _pick(t, (768, 512, 256))
    te = 1024
    a_per = te // PEER_NKEYS
    sel_spec = pl.BlockSpec((PEER_HEADS, a_per, tm), lambda i, j: (0, j, i))
    tok_spec = pl.BlockSpec((PEER_HEADS, PEER_NKEYS, tm), lambda i, j: (0, 0, i))
    return pl.pallas_call(
        _peer_main_kernel,
        out_shape=jax.ShapeDtypeStruct((d, t), F32),
        grid=(t // tm, n_exp // te),
        in_specs=[pl.BlockSpec((d, tm), lambda i, j: (0, i)),
                  pl.BlockSpec((te, d), lambda i, j: (j, 0)),
                  pl.BlockSpec((d, te), lambda i, j: (0, j)),
                  sel_spec, sel_spec, tok_spec, tok_spec],
        out_specs=pl.BlockSpec((d, tm), lambda i, j: (0, i)),
        compiler_params=_cparams(("parallel", "arbitrary")),
        name="peer_main",
    )(h2t, u, vt, th, ka, s2, e2)


def _final_kernel(x1_ref, yt_ref, g2p_ref, g2s_ref, op_ref, os_ref, *, n_prompt_blocks):
    i = pl.program_id(0)
    y = yt_ref[...].T

    @pl.when(i < n_prompt_blocks)
    def _():
        op_ref[...] = x1_ref[...] + g2p_ref[...] * y

    @pl.when(i >= n_prompt_blocks)
    def _():
        os_ref[...] = x1_ref[...] + g2s_ref[...] * y


def _final(x1, yt, g2p, g2s, tp):
    t, d = x1.shape
    ts = t - tp
    tb = ROW_BLOCK
    npb, nb = tp // tb, t // tb
    pidx = lambda i: (jnp.minimum(i, npb - 1), 0)
    sidx = lambda i: (jnp.maximum(i - npb, 0), 0)
    return pl.pallas_call(
        functools.partial(_final_kernel, n_prompt_blocks=npb),
        out_shape=(jax.ShapeDtypeStruct((tp, d), F32), jax.ShapeDtypeStruct((ts, d), F32)),
        grid=(nb,),
        in_specs=[pl.BlockSpec((tb, d), lambda i: (i, 0)), pl.BlockSpec((d, tb), lambda i: (0, i)),
                  pl.BlockSpec((1, d), lambda i: (0, 0)), pl.BlockSpec((tb, d), sidx)],
        out_specs=(pl.BlockSpec((tb, d), pidx), pl.BlockSpec((tb, d), sidx)),
        compiler_params=_cparams(("arbitrary",)),
        name="final_residual",
    )(x1, yt, g2p, g2s)


def _layer(layer, xp, xs, c_prompt, c_sample, state_gla, cache_k, cache_v, cache_logf, page_table,
           norm1_g, norm2_g, w_ada, b_ada, w_in, gla_f2, gla_fb, gla_on_g, fox_qn_g, fox_kn_g, fox_fb,
           w_pa, w_pb, w_out, peer_wq, peer_subkeys, peer_u, peer_v):
    bp, tpp, d = xp.shape
    assert bp == 1 and d == D_MODEL
    b, n_new, _ = xs.shape
    tp, ts = bp * tpp, b * n_new
    assert tp % ROW_BLOCK == 0 and ts % ROW_BLOCK == 0
    xp2 = xp.reshape(tp, d)
    xs2 = xs.reshape(ts, d)

    n_c = 1 + b
    n_c_pad = -(-n_c // 8) * 8
    c_all = jnp.concatenate([c_prompt, c_sample, jnp.zeros((n_c_pad - n_c, d), F32)], axis=0)
    mod = _ada(c_all, w_ada, b_ada)
    mod_p = [mod[0:1, i * d:(i + 1) * d] for i in range(6)]
    mod_s = [jnp.repeat(mod[1:n_c, i * d:(i + 1) * d], n_new, axis=0) for i in range(6)]
    sh1p, sc1p, g1p, sh2p, sc2p, g2p = mod_p
    sh1s, sc1s, g1s, sh2s, sc2s, g2s = mod_s

    h = _normmod(xp2, xs2, norm1_g.reshape(1, d), sc1p, sh1p, sc1s, sh1s)

    w1 = jnp.concatenate([w_in[:, :_OFF_GF], w_in[:, _OFF_FQ:_OFF_FK]], axis=1).astype(BF16)
    w2 = w_in[:, _OFF_FK:_OFF_FF].astype(BF16)
    w3 = w_in[:, _OFF_MA:_OFF_END].astype(BF16)
    w4 = jnp.concatenate([w_in[:, _OFF_FF:_OFF_MA], w_in[:, _OFF_GF:_OFF_FQ],
                          jnp.zeros((d, LANES - FOX_HEADS - GLA_GATE_RANK), F32)], axis=1).astype(BF16)
    z1 = _matmul(h, w1, BF16, "in_proj_mixers")
    z2 = _matmul(h, w2, F32, "in_proj_fox_kv")
    z3 = _matmul(h, w3, BF16, "in_proj_gates")
    zs = _matmul(h, w4, F32, "in_proj_small")

    fb_pad = jnp.concatenate([fox_fb, jnp.zeros((LANES - FOX_HEADS,), F32)]).reshape(1, LANES)
    (qn, kb, vb, cum, k_p, v_p, lf_p, k_s, v_s, lf_s) = _fox_prep(
        z1, z2, zs, fox_qn_g.reshape(1, FOX_HD), fox_kn_g.reshape(1, FOX_HD), fb_pad, tp, n_new)
    neg_cum_t = (-LOG2E * cum[:tp, :FOX_HEADS]).T.reshape(FOX_HEADS, 1, tp)
    of_p = _fox_prompt(qn, kb, vb, neg_cum_t, tp)

    qs = qn[tp:].reshape(b, n_new, FOX_HEADS, FOX_HD)
    eye = jnp.eye(FOX_HEADS, dtype=BF16)
    qbd = (qs.transpose(0, 2, 3, 1)[:, :, :, None, :] * eye[None, :, None, :, None])
    qbd = qbd.reshape(b, FOX_W, FOX_HEADS * n_new)
    qbd = jnp.concatenate([qbd, jnp.zeros((b, FOX_W, LANES - FOX_HEADS * n_new), BF16)], axis=2)
    lfn_t = lf_s[:, :FOX_HEADS].reshape(b, n_new, FOX_HEADS).transpose(0, 2, 1)
    lfn_t = jnp.concatenate([lfn_t, jnp.zeros((b, FOX_HEADS, LANES - n_new), F32)], axis=2)
    of_s = _fox_sample(layer, page_table, qbd, k_s.reshape(b, n_new, FOX_W), v_s.reshape(b, n_new, FOX_W), lfn_t,
                       cache_k, cache_v, cache_logf).reshape(ts, FOX_W).astype(BF16)

    f2p = jnp.zeros((LANES, GLA_QK_W), F32).at[FOX_HEADS:FOX_HEADS + GLA_GATE_RANK].set(gla_f2)
    f2p = f2p.reshape(LANES, GLA_HEADS, GLA_DK).transpose(1, 0, 2).astype(BF16)
    fbh = gla_fb.reshape(GLA_HEADS, 1, GLA_DK)
    ong = gla_on_g.reshape(1, GLA_DV)
    og_p, sg_p = _gla_prompt(z1, zs, f2p, fbh, ong, jnp.zeros((GLA_HEADS, GLA_DK, GLA_DV), F32), tp)
    og_s, sg_s = _gla_sample(z1[tp:].astype(F32), zs[tp:], f2p, fbh, ong, state_gla, n_new)

    merged = _merge(og_p, og_s.astype(BF16), of_p, of_s, z3, w_pa.astype(BF16), w_pb.astype(BF16))
    x1, h2 = _outproj(merged, w_out.astype(BF16), xp2, xs2, g1p, g1s, norm2_g.reshape(1, d),
                      sc2p, sh2p, sc2s, sh2s)

    q = _matmul(h2, peer_wq.astype(BF16), BF16, "peer_query")
    sk = peer_subkeys.reshape(PEER_HEADS * 2, PEER_NKEYS, -1).astype(BF16)
    th, ka, s2, e2 = _peer_topk(q, sk)
    yt = _peer_main(h2.T, peer_u.astype(BF16), peer_v.astype(BF16).T, th, ka, s2, e2)
    out_p, out_s = _final(x1, yt, g2p, g2s, tp)

    return (out_p.reshape(bp, tpp, d), out_s.reshape(b, n_new, d),
            sg_p.reshape(bp, GLA_HEADS, GLA_DK, GLA_DV), sg_s,
            k_p.reshape(bp, tpp, FOX_HEADS, FOX_HD), v_p.reshape(bp, tpp, FOX_HEADS, FOX_HD),
            lf_p[:, :FOX_HEADS].reshape(bp, tpp, FOX_HEADS),
            k_s.reshape(b, n_new, FOX_HEADS, FOX_HD), v_s.reshape(b, n_new, FOX_HEADS, FOX_HD),
            lf_s[:, :FOX_HEADS].reshape(b, n_new, FOX_HEADS))


def kernel(x_prompt, x_sample, c_prompt, c_sample, state_gla, cache_k, cache_v, cache_logf, page_table,
           norm1_g, norm2_g, w_ada, b_ada, w_in, gla_f2, gla_fb, gla_on_g, fox_qn_g, fox_kn_g, fox_fb,
           w_pa, w_pb, w_out, peer_wq, peer_subkeys, peer_u, peer_v):
    depth = w_in.shape[0]
    xp, xs = x_prompt, x_sample
    outs = []
    for l in range(depth):
        res = _layer(l, xp, xs, c_prompt, c_sample, state_gla[l], cache_k, cache_v, cache_logf, page_table,
                     norm1_g[l], norm2_g[l], w_ada[l], b_ada[l], w_in[l], gla_f2[l], gla_fb[l], gla_on_g[l],
                     fox_qn_g[l], fox_kn_g[l], fox_fb[l], w_pa[l], w_pb[l], w_out[l],
                     peer_wq[l], peer_subkeys[l], peer_u[l], peer_v[l])
        xp, xs = res[0], res[1]
        outs.append(res[2:])
    stacked = [jnp.stack([o[i] for o in outs]) for i in range(8)]
    return (xp, xs, *stacked)
```

```python
import functools

import jax
import jax.numpy as jnp
from jax import lax
from jax.experimental import pallas as pl
from jax.experimental.pallas import tpu as pltpu

F32 = jnp.float32
BF16 = jnp.bfloat16

D_MODEL = 2048
GLA_HEADS = 4
GLA_DK = 128
GLA_DV = 256
GLA_GATE_RANK = 16
GLA_GATE_TAU = 16.0
GLA_CHUNK = 64
GLA_ROWBLOCK = 16
FOX_HEADS = 8
FOX_HD = 128
FOX_W = FOX_HEADS * FOX_HD
PEER_HEADS = 8
PEER_NKEYS = 128
PEER_TOPK = 16
PAGE_SIZE = 128
EPS = 1e-6

GLA_QK_W = GLA_HEADS * GLA_DK
GLA_V_W = GLA_HEADS * GLA_DV
_OFF_GF = 2 * GLA_QK_W + 2 * GLA_V_W
_OFF_FQ = _OFF_GF + GLA_GATE_RANK
_OFF_FK = _OFF_FQ + FOX_W
_OFF_FV = _OFF_FK + FOX_W
_OFF_FF = _OFF_FV + FOX_W
_OFF_MA = _OFF_FF + FOX_HEADS
_OFF_MB = _OFF_MA + D_MODEL
_OFF_END = _OFF_MB + D_MODEL

LANES = 128
ROW_BLOCK = 256
VMEM_LIMIT = 56 * 1024 * 1024

NT = (((1,), (1,)), ((), ()))
NEG_INF = float("-inf")
LOG2E = 1.4426950408889634


def _cparams(sem):
    return pltpu.CompilerParams(dimension_semantics=sem, vmem_limit_bytes=VMEM_LIMIT)


def _split3(x):
    hi = x.astype(BF16)
    r = x - hi.astype(F32)
    mid = r.astype(BF16)
    lo = (r - mid.astype(F32)).astype(BF16)
    return hi, mid, lo


def _dot01(m01, x):
    acc = None
    for p in _split3(x):
        t = jnp.dot(m01, p, preferred_element_type=F32)
        acc = t if acc is None else acc + t
    return acc


def _dotx01(x, m01):
    acc = None
    for p in _split3(x):
        t = jnp.dot(p, m01, preferred_element_type=F32)
        acc = t if acc is None else acc + t
    return acc


def _log_sigmoid(z):
    return jnp.minimum(z, 0.0) - jnp.log1p(jnp.exp(-jnp.abs(z)))


def _rms(x, g):
    return x * lax.rsqrt(jnp.mean(x * x, axis=-1, keepdims=True) + EPS) * g


def _ada_kernel(c_ref, w_ref, b_ref, o_ref):
    o_ref[...] = jnp.dot(c_ref[...].astype(BF16), w_ref[...].astype(BF16),
                         preferred_element_type=F32) + b_ref[...]


def _ada(c_all, w_ada, b_ada):
    m, k = c_all.shape
    n = w_ada.shape[1]
    tn = 1024
    return pl.pallas_call(
        _ada_kernel,
        out_shape=jax.ShapeDtypeStruct((m, n), F32),
        grid=(n // tn,),
        in_specs=[pl.BlockSpec((m, k), lambda j: (0, 0)),
                  pl.BlockSpec((k, tn), lambda j: (0, j)),
                  pl.BlockSpec((1, tn), lambda j: (0, j))],
        out_specs=pl.BlockSpec((m, tn), lambda j: (0, j)),
        compiler_params=_cparams(("parallel",)),
        name="ada_mod",
    )(c_all, w_ada, b_ada.reshape(1, n))


def _normmod_kernel(xp_ref, xs_ref, g_ref, scp_ref, shp_ref, scs_ref, shs_ref, o_ref, *, n_prompt_blocks):
    i = pl.program_id(0)

    @pl.when(i < n_prompt_blocks)
    def _():
        h = _rms(xp_ref[...], g_ref[...]) * (1.0 + scp_ref[...]) + shp_ref[...]
        o_ref[...] = h.astype(o_ref.dtype)

    @pl.when(i >= n_prompt_blocks)
    def _():
        h = _rms(xs_ref[...], g_ref[...]) * (1.0 + scs_ref[...]) + shs_ref[...]
        o_ref[...] = h.astype(o_ref.dtype)


def _normmod(xp, xs, g, scp, shp, scs, shs):
    tp, d = xp.shape
    ts = xs.shape[0]
    tb = ROW_BLOCK
    npb, nsb = tp // tb, ts // tb
    pidx = lambda i: (jnp.minimum(i, npb - 1), 0)
    sidx = lambda i: (jnp.maximum(i - npb, 0), 0)
    row = pl.BlockSpec((1, d), lambda i: (0, 0))
    return pl.pallas_call(
        functools.partial(_normmod_kernel, n_prompt_blocks=npb),
        out_shape=jax.ShapeDtypeStruct((tp + ts, d), BF16),
        grid=(npb + nsb,),
        in_specs=[pl.BlockSpec((tb, d), pidx), pl.BlockSpec((tb, d), sidx), row, row, row,
                  pl.BlockSpec((tb, d), sidx), pl.BlockSpec((tb, d), sidx)],
        out_specs=pl.BlockSpec((tb, d), lambda i: (i, 0)),
        compiler_params=_cparams(("parallel",)),
        name="norm_mod",
    )(xp, xs, g, scp, shp, scs, shs)


def _mm_kernel(a_ref, w_ref, o_ref):
    o_ref[...] = jnp.dot(a_ref[...], w_ref[...], preferred_element_type=F32).astype(o_ref.dtype)


def _pick(n, cands):
    for c in cands:
        if n % c == 0:
            return c
    return n


def _matmul(a, w, out_dtype, name):
    m, k = a.shape
    n = w.shape[1]
    tm = _pick(m, (1024, 768, 512, 256))
    tn = _pick(n, (1024, 512, 256, 128))
    return pl.pallas_call(
        _mm_kernel,
        out_shape=jax.ShapeDtypeStruct((m, n), out_dtype),
        grid=(m // tm, n // tn),
        in_specs=[pl.BlockSpec((tm, k), lambda i, j: (i, 0)),
                  pl.BlockSpec((k, tn), lambda i, j: (0, j))],
        out_specs=pl.BlockSpec((tm, tn), lambda i, j: (i, j)),
        compiler_params=_cparams(("parallel", "parallel")),
        name=name,
    )(a, w)


def _fox_prep_kernel(fq_ref, fkv_ref, zs_ref, qg_ref, kg_ref, fb_ref,
                     qn_ref, kb_ref, vb_ref, cum_ref, kp_ref, vp_ref, lfp_ref, ks_ref, vs_ref, lfs_ref,
                     carry_ref, *, n_prompt_blocks, seg_len):
    i = pl.program_id(0)
    tb = fq_ref.shape[0]
    is_prompt = i < n_prompt_blocks

    @pl.when(i == 0)
    def _():
        carry_ref[...] = jnp.zeros_like(carry_ref)

    scale = FOX_HD ** -0.5 * jnp.where(is_prompt, LOG2E, 1.0).astype(F32)
    fq = fq_ref[...].astype(F32)
    fkv = fkv_ref[...]
    kn = []
    for h in range(FOX_HEADS):
        sl = slice(h * FOX_HD, (h + 1) * FOX_HD)
        qn_ref[:, sl] = (_rms(fq[:, sl], qg_ref[...]) * scale).astype(qn_ref.dtype)
        kn.append(_rms(fkv[:, sl], kg_ref[...]))
    kn = jnp.concatenate(kn, axis=1)
    v = fkv[:, FOX_W:]
    kb_ref[...] = kn.astype(kb_ref.dtype)
    vb_ref[...] = v.astype(vb_ref.dtype)

    lf = _log_sigmoid(zs_ref[...] + fb_ref[...])
    row = lax.broadcasted_iota(jnp.int32, (tb, tb), 0)
    col = lax.broadcasted_iota(jnp.int32, (tb, tb), 1)

    @pl.when(is_prompt)
    def _():
        tri = jnp.where(col <= row, 1.0, 0.0).astype(BF16)
        cum = _dot01(tri, lf) + carry_ref[...]
        carry_ref[...] = cum[tb - 1:tb, :]
        cum_ref[...] = cum
        kp_ref[...] = kn
        vp_ref[...] = v
        lfp_ref[...] = lf

    @pl.when(jnp.logical_not(is_prompt))
    def _():
        tri = jnp.where(col <= row, jnp.where((row // seg_len) == (col // seg_len), 1.0, 0.0), 0.0).astype(BF16)
        cum_ref[...] = _dot01(tri, lf)
        ks_ref[...] = kn
        vs_ref[...] = v
        lfs_ref[...] = lf


def _fox_prep(z1, z2, zs, qg, kg, fb_pad, tp, seg_len):
    t_all = z1.shape[0]
    ts = t_all - tp
    tb = ROW_BLOCK
    npb, nb = tp // tb, t_all // tb
    pidx = lambda i: (jnp.minimum(i, npb - 1), 0)
    sidx = lambda i: (jnp.maximum(i - npb, 0), 0)
    allidx = lambda i: (i, 0)
    row = pl.BlockSpec((1, LANES), lambda i: (0, 0))
    fq_block = (_OFF_GF) // FOX_W
    return pl.pallas_call(
        functools.partial(_fox_prep_kernel, n_prompt_blocks=npb, seg_len=seg_len),
        out_shape=(jax.ShapeDtypeStruct((t_all, FOX_W), BF16),
                   jax.ShapeDtypeStruct((t_all, FOX_W), BF16),
                   jax.ShapeDtypeStruct((t_all, FOX_W), BF16),
                   jax.ShapeDtypeStruct((t_all, LANES), F32),
                   jax.ShapeDtypeStruct((tp, FOX_W), F32),
                   jax.ShapeDtypeStruct((tp, FOX_W), F32),
                   jax.ShapeDtypeStruct((tp, LANES), F32),
                   jax.ShapeDtypeStruct((ts, FOX_W), F32),
                   jax.ShapeDtypeStruct((ts, FOX_W), F32),
                   jax.ShapeDtypeStruct((ts, LANES), F32)),
        grid=(nb,),
        in_specs=[pl.BlockSpec((tb, FOX_W), lambda i: (i, fq_block)),
                  pl.BlockSpec((tb, 2 * FOX_W), allidx),
                  pl.BlockSpec((tb, LANES), allidx), row, row, row],
        out_specs=(pl.BlockSpec((tb, FOX_W), allidx), pl.BlockSpec((tb, FOX_W), allidx),
                   pl.BlockSpec((tb, FOX_W), allidx), pl.BlockSpec((tb, LANES), allidx),
                   pl.BlockSpec((tb, FOX_W), pidx), pl.BlockSpec((tb, FOX_W), pidx),
                   pl.BlockSpec((tb, LANES), pidx),
                   pl.BlockSpec((tb, FOX_W), sidx), pl.BlockSpec((tb, FOX_W), sidx),
                   pl.BlockSpec((tb, LANES), sidx)),
        scratch_shapes=[pltpu.VMEM((1, LANES), F32)],
        compiler_params=_cparams(("arbitrary",)),
        name="fox_prep",
    )(z1, z2, zs, qg, kg, fb_pad)


FOX_HEADS_PER_STEP = 4


def _fox_prompt_kernel(q_ref, k_ref, v_ref, nc_ref, o_ref, *, tk):
    qi = pl.program_id(1)
    tq = q_ref.shape[0]
    nh = q_ref.shape[1] // FOX_HD
    n_full = (qi * tq) // tk
    n_diag = -(-tq // tk)

    def block(c0, masked, carry):
        out = []
        for h in range(nh):
            m, l, acc = carry[h]
            sl = slice(h * FOX_HD, (h + 1) * FOX_HD)
            s = lax.dot_general(q_ref[:, sl], k_ref[pl.ds(c0, tk), sl], NT, preferred_element_type=F32)
            s = s + nc_ref[h, :, pl.ds(c0, tk)]
            if masked:
                row = qi * tq + lax.broadcasted_iota(jnp.int32, (tq, tk), 0)
                col = c0 + lax.broadcasted_iota(jnp.int32, (tq, tk), 1)
                s = jnp.where(col <= row, s, NEG_INF)
            m_new = jnp.maximum(m, jnp.max(s, axis=1, keepdims=True))
            alpha = jnp.exp2(m - m_new)
            p = jnp.exp2(s - m_new)
            l = alpha * l + jnp.sum(p, axis=1, keepdims=True)
            acc = alpha * acc + jnp.dot(p.astype(BF16), v_ref[pl.ds(c0, tk), sl], preferred_element_type=F32)
            out.append((m_new, l, acc))
        return tuple(out)

    init = tuple((jnp.full((tq, 1), NEG_INF, F32), jnp.zeros((tq, 1), F32), jnp.zeros((tq, FOX_HD), F32))
                 for _ in range(nh))
    carry = lax.fori_loop(0, n_full, lambda i, c: block(pl.multiple_of(i * tk, tk), False, c), init)
    final = lax.fori_loop(n_full, n_full + n_diag, lambda i, c: block(pl.multiple_of(i * tk, tk), True, c), carry)
    for h in range(nh):
        _, l, acc = final[h]
        o_ref[:, h * FOX_HD:(h + 1) * FOX_HD] = (acc / l).astype(o_ref.dtype)


def _fox_prompt(qn, kb, vb, neg_cum_t, tp):
    tk = _pick(tp, (512, 256, 128))
    tq = tk
    nh = FOX_HEADS_PER_STEP
    w = nh * FOX_HD
    return pl.pallas_call(
        functools.partial(_fox_prompt_kernel, tk=tk),
        out_shape=jax.ShapeDtypeStruct((tp, FOX_W), BF16),
        grid=(FOX_HEADS // nh, tp // tq),
        in_specs=[pl.BlockSpec((tq, w), lambda g, i: (i, g)),
                  pl.BlockSpec((tp, w), lambda g, i: (0, g)),
                  pl.BlockSpec((tp, w), lambda g, i: (0, g)),
                  pl.BlockSpec((nh, 1, tp), lambda g, i: (g, 0, 0))],
        out_specs=pl.BlockSpec((tq, w), lambda g, i: (i, g)),
        compiler_params=_cparams(("parallel", "parallel")),
        name="fox_prompt",
    )(qn, kb, vb, neg_cum_t)


def _expand_heads(x_t, n_q):
    hi, mid, lo = _split3(x_t)
    n = x_t.shape[0]
    stacked = jnp.concatenate([hi.astype(F32), mid.astype(F32), lo.astype(F32),
                               jnp.zeros((LANES - 3 * n, LANES), F32)], axis=0)
    k = lax.broadcasted_iota(jnp.int32, (LANES, LANES), 0)
    c = lax.broadcasted_iota(jnp.int32, (LANES, LANES), 1)
    e3 = jnp.where(k < 3 * n, jnp.where(c < n * n_q, jnp.where((k % n) == (c // n_q), 1.0, 0.0), 0.0), 0.0)
    return jnp.dot(stacked.T.astype(BF16), e3.astype(BF16), preferred_element_type=F32)


def _page_to_rows(page_ref, buf_ref, p):
    for h in range(FOX_HEADS):
        x = page_ref[pl.ds(h, PAGE_SIZE, stride=FOX_HEADS), :]
        buf_ref[p * PAGE_SIZE:(p + 1) * PAGE_SIZE, h * FOX_HD:(h + 1) * FOX_HD] = x.astype(BF16)


def _fox_sample_kernel(pt_ref, qbd_ref, kn_ref, vn_ref, lfn_ref, *rest, pages_per_step, n_groups):
    pg = pages_per_step
    k_refs = rest[:pg]
    lf_refs = rest[pg:2 * pg]
    v_refs = rest[2 * pg:3 * pg]
    o_ref = rest[3 * pg]
    s_all, s_new, m_ref, l_ref, acc_ref, carry_ref, kv_buf, eall_ref = rest[3 * pg + 1:]
    j = pl.program_id(1)
    n_new = kn_ref.shape[0]
    hp = FOX_HEADS
    rows = pg * PAGE_SIZE
    ki = lax.broadcasted_iota(jnp.int32, (LANES, LANES), 0)
    ci = lax.broadcasted_iota(jnp.int32, (LANES, LANES), 1)

    @pl.when(j == 0)
    def _():
        pad = jnp.zeros((LANES - n_new, FOX_W), F32)
        knp = jnp.concatenate([kn_ref[...], pad], axis=0).astype(BF16)
        incl = jnp.where(ki <= ci, 1.0, 0.0).astype(BF16)
        cn_t = _dotx01(lfn_ref[...], incl)
        s = jnp.dot(knp, qbd_ref[...], preferred_element_type=F32) - _expand_heads(cn_t, n_new)
        s = jnp.where(ki <= ci % n_new, s, NEG_INF)
        s = jnp.where(ki < n_new, s, NEG_INF)
        s = jnp.where(ci < FOX_HEADS * n_new, s, NEG_INF)
        s_new[...] = s
        m_ref[...] = jnp.max(s, axis=0, keepdims=True)
        carry_ref[...] = jnp.zeros_like(carry_ref)
        kk = lax.broadcasted_iota(jnp.int32, eall_ref.shape, 0)
        cc = lax.broadcasted_iota(jnp.int32, eall_ref.shape, 1)
        hit = kk == (cc // LANES) * hp + (cc % LANES) // n_new
        eall_ref[...] = jnp.where(hit, jnp.where(cc % LANES < FOX_HEADS * n_new, 1.0, 0.0), 0.0).astype(BF16)

    @pl.when(j < n_groups)
    def _():
        g = n_groups - 1 - j
        for p in range(pg):
            _page_to_rows(k_refs[p], kv_buf, p)
        s = jnp.dot(kv_buf[...], qbd_ref[...], preferred_element_type=F32)
        tiles = [lf_refs[p][...] for p in range(pg)]
        if pg * hp < LANES:
            tiles.append(jnp.zeros((LANES - pg * hp, PAGE_SIZE), F32))
        x = jnp.concatenate(tiles, axis=0).T
        later_t = jnp.where(ci > ki, 1.0, 0.0).astype(BF16)
        same_head = (ki % hp) == (ci % hp)
        later_page = jnp.where(same_head, jnp.where(ki // hp > ci // hp, 1.0, 0.0), 0.0).astype(BF16)
        any_page = jnp.where(same_head, 1.0, 0.0).astype(BF16)
        tot = jnp.broadcast_to(jnp.sum(x, axis=0, keepdims=True), (8, LANES))
        tail = _dot01(later_t, x) + (_dotx01(tot, later_page)[0:1] + carry_ref[...])
        carry_ref[...] = carry_ref[...] + _dotx01(tot, any_page)[0:1]
        ex = jnp.dot(jnp.concatenate(list(_split3(tail)), axis=0), eall_ref[...], preferred_element_type=F32)
        blocks = []
        for p in range(pg):
            cols = slice(p * LANES, (p + 1) * LANES)
            add = ex[0:LANES, cols] + ex[LANES:2 * LANES, cols] + ex[2 * LANES:3 * LANES, cols]
            blocks.append(s[p * PAGE_SIZE:(p + 1) * PAGE_SIZE] + add)
        s = jnp.concatenate(blocks, axis=0)
        s_all[pl.ds(pl.multiple_of(g * rows, rows), rows), :] = s
        m_ref[...] = jnp.maximum(m_ref[...], jnp.max(s, axis=0, keepdims=True))

    @pl.when(j == n_groups)
    def _():
        pad = jnp.zeros((LANES - n_new, FOX_W), F32)
        vnp = jnp.concatenate([vn_ref[...], pad], axis=0).astype(BF16)
        p = jnp.exp(s_new[...] - m_ref[...])
        l_ref[...] = jnp.sum(p, axis=0, keepdims=True)
        acc_ref[...] = jnp.dot(p.T.astype(BF16), vnp, preferred_element_type=F32)

    @pl.when(j >= n_groups)
    def _():
        g = 2 * n_groups - 1 - j
        for p in range(pg):
            _page_to_rows(v_refs[p], kv_buf, p)
        pexp = jnp.exp(s_all[pl.ds(pl.multiple_of(g * rows, rows), rows), :] - m_ref[...])
        l_ref[...] += jnp.sum(pexp, axis=0, keepdims=True)
        acc_ref[...] += jnp.dot(pexp.T.astype(BF16), kv_buf[...], preferred_element_type=F32)

    @pl.when(j == 2 * n_groups - 1)
    def _():
        l_col = jnp.broadcast_to(l_ref[...], (LANES, LANES)).T
        for h in range(FOX_HEADS):
            rows = slice(h * n_new, (h + 1) * n_new)
            cols = slice(h * FOX_HD, (h + 1) * FOX_HD)
            o_ref[:, cols] = acc_ref[rows, cols] / l_col[rows, :]


def _fox_sample(layer, page_table, qbd, kn, vn, lfn_t, ck, cv, clf):
    ck = ck.reshape(ck.shape[0], ck.shape[1], PAGE_SIZE * FOX_HEADS, FOX_HD)
    cv = cv.reshape(cv.shape[0], cv.shape[1], PAGE_SIZE * FOX_HEADS, FOX_HD)
    clf_t = clf.transpose(0, 1, 3, 2)
    b, n_pages = page_table.shape
    n_new = kn.shape[1]
    pg = _pick(n_pages, (16, 8, 4, 2))
    ng = n_pages // pg
    assert FOX_HEADS * pg <= LANES and FOX_HEADS * n_new <= LANES

    def page_map(r, phase_v, n_trailing):
        def index(bi, j, pt):
            step = jnp.maximum(j - ng, 0) if phase_v else jnp.minimum(j, ng - 1)
            return (layer, pt[bi, (ng - 1 - step) * pg + r]) + (0,) * n_trailing
        return index

    per_b = lambda bi, j, pt: (bi, 0, 0)
    page_blk = (None, None, PAGE_SIZE * FOX_HEADS, FOX_HD)
    in_specs = [pl.BlockSpec((None, FOX_W, LANES), per_b),
                pl.BlockSpec((None, n_new, FOX_W), per_b),
                pl.BlockSpec((None, n_new, FOX_W), per_b),
                pl.BlockSpec((None, FOX_HEADS, LANES), per_b)]
    in_specs += [pl.BlockSpec(page_blk, page_map(r, False, 2)) for r in range(pg)]
    in_specs += [pl.BlockSpec((None, None, FOX_HEADS, PAGE_SIZE), page_map(r, False, 2)) for r in range(pg)]
    in_specs += [pl.BlockSpec(page_blk, page_map(r, True, 2)) for r in range(pg)]
    grid_spec = pltpu.PrefetchScalarGridSpec(
        num_scalar_prefetch=1,
        grid=(b, 2 * ng),
        in_specs=in_specs,
        out_specs=pl.BlockSpec((None, n_new, FOX_W), per_b),
        scratch_shapes=[pltpu.VMEM((n_pages * PAGE_SIZE, LANES), F32),
                        pltpu.VMEM((LANES, LANES), F32),
                        pltpu.VMEM((1, LANES), F32),
                        pltpu.VMEM((1, LANES), F32),
                        pltpu.VMEM((LANES, FOX_W), F32),
                        pltpu.VMEM((1, LANES), F32),
                        pltpu.VMEM((pg * PAGE_SIZE, FOX_W), BF16),
                        pltpu.VMEM((LANES, pg * LANES), BF16)],
    )
    return pl.pallas_call(
        functools.partial(_fox_sample_kernel, pages_per_step=pg, n_groups=ng),
        out_shape=jax.ShapeDtypeStruct((b, n_new, FOX_W), F32),
        grid_spec=grid_spec,
        compiler_params=_cparams(("parallel", "arbitrary")),
        name="fox_sample",
    )(page_table, qbd, kn, vn, lfn_t, *([ck] * pg), *([clf_t] * pg), *([cv] * pg))


def _gla_chunk(q, k, v, zs, f2p, fb, s_state, n_row_blocks, rb):
    c = q.shape[0]
    z = jnp.dot(zs.astype(BF16), f2p, preferred_element_type=F32) + fb
    la = _log_sigmoid(z) * (1.0 / GLA_GATE_TAU)
    ri = lax.broadcasted_iota(jnp.int32, (c, c), 0)
    ci = lax.broadcasted_iota(jnp.int32, (c, c), 1)
    b = _dot01(jnp.where(ci <= ri, 1.0, 0.0).astype(BF16), la)
    bl = b[c - 1:c]
    rbi = lax.broadcasted_iota(jnp.int32, (rb, c), 0)
    cbi = lax.broadcasted_iota(jnp.int32, (rb, c), 1)
    rows = []
    for i in range(n_row_blocks):
        r0 = i * rb
        qb, kb_, bb = q[r0:r0 + rb], k[r0:r0 + rb], b[r0:r0 + rb]
        a = jnp.zeros((rb, c), F32)
        if i > 0:
            bref = b[r0:r0 + 1]
            qt = qb * jnp.exp(bb - bref)
            kt = k * jnp.exp(jnp.minimum(bref - b, 0.0))
            a = lax.dot_general(qt.astype(BF16), kt.astype(BF16), NT, preferred_element_type=F32)
            a = jnp.where(cbi < r0, a, 0.0)
        for s in range(rb):
            e = jnp.exp(jnp.minimum(bb - bb[s:s + 1], 0.0))
            colv = jnp.sum(qb * kb_[s:s + 1] * e, axis=-1, keepdims=True)
            a = jnp.where(cbi == r0 + s, jnp.where(rbi >= s, colv, 0.0), a)
        rows.append(a)
    a = rows[0] if len(rows) == 1 else jnp.concatenate(rows, axis=0)
    nr = n_row_blocks * rb
    o = (jnp.dot(a.astype(BF16), v, preferred_element_type=F32)
         + jnp.dot((q[:nr] * jnp.exp(b[:nr])).astype(BF16), s_state.astype(BF16), preferred_element_type=F32))
    kh = (k * jnp.exp(bl - b)).astype(BF16)
    blc = jnp.exp(jnp.broadcast_to(bl, (GLA_DK, GLA_DK)).T)
    upd = lax.dot_general(kh, v, (((0,), (0,)), ((), ())), preferred_element_type=F32)
    s_next = jnp.concatenate([blc] * (GLA_DV // GLA_DK), axis=1) * s_state + upd
    return o, s_next


def _gla_out(o, gr, ong):
    return _rms(o, ong) * (gr * jax.nn.sigmoid(gr))


def _gla_prompt_kernel(q_ref, k_ref, v_ref, gr_ref, zs_ref, f2p_ref, fb_ref, ong_ref, s0_ref,
                       og_ref, sout_ref, s_ref):
    i = pl.program_id(0)
    tb = q_ref.shape[0]
    c = GLA_CHUNK

    @pl.when(i == 0)
    def _():
        s_ref[...] = s0_ref[...]

    def chunk(n, carry):
        r = pl.multiple_of(n * c, c)
        zs = zs_ref[pl.ds(r, c), :]
        for h in range(GLA_HEADS):
            qk = slice(h * GLA_DK, (h + 1) * GLA_DK)
            vv = slice(h * GLA_DV, (h + 1) * GLA_DV)
            q = q_ref[pl.ds(r, c), qk].astype(F32) * (GLA_DK ** -0.5)
            k = k_ref[pl.ds(r, c), qk].astype(F32)
            o, s_next = _gla_chunk(q, k, v_ref[pl.ds(r, c), vv], zs, f2p_ref[h], fb_ref[h],
                                   s_ref[h], c // GLA_ROWBLOCK, GLA_ROWBLOCK)
            s_ref[h] = s_next
            og_ref[pl.ds(r, c), vv] = _gla_out(o, gr_ref[pl.ds(r, c), vv].astype(F32),
                                               ong_ref[...]).astype(og_ref.dtype)
        return carry

    lax.fori_loop(0, tb // c, chunk, 0)

    @pl.when(i == pl.num_programs(0) - 1)
    def _():
        sout_ref[...] = s_ref[...]


def _gla_prompt(z1, zs, f2p, fb, ong, s0, tp):
    tb = _pick(tp, (512, 256, 128, 64))
    whole = lambda i: (0, 0, 0)
    return pl.pallas_call(
        _gla_prompt_kernel,
        out_shape=(jax.ShapeDtypeStruct((tp, GLA_V_W), BF16),
                   jax.ShapeDtypeStruct((GLA_HEADS, GLA_DK, GLA_DV), F32)),
        grid=(tp // tb,),
        in_specs=[pl.BlockSpec((tb, GLA_QK_W), lambda i: (i, 0)),
                  pl.BlockSpec((tb, GLA_QK_W), lambda i: (i, 1)),
                  pl.BlockSpec((tb, GLA_V_W), lambda i: (i, 1)),
                  pl.BlockSpec((tb, GLA_V_W), lambda i: (i, 2)),
                  pl.BlockSpec((tb, LANES), lambda i: (i, 0)),
                  pl.BlockSpec((GLA_HEADS, LANES, GLA_DK), whole),
                  pl.BlockSpec((GLA_HEADS, 1, GLA_DK), whole),
                  pl.BlockSpec((1, GLA_DV), lambda i: (0, 0)),
                  pl.BlockSpec((GLA_HEADS, GLA_DK, GLA_DV), whole)],
        out_specs=(pl.BlockSpec((tb, GLA_V_W), lambda i: (i, 0)),
                   pl.BlockSpec((GLA_HEADS, GLA_DK, GLA_DV), whole)),
        scratch_shapes=[pltpu.VMEM((GLA_HEADS, GLA_DK, GLA_DV), F32)],
        compiler_params=_cparams(("arbitrary",)),
        name="gla_prompt",
    )(z1, z1, z1, z1, zs, f2p, fb, ong, s0)


def _gla_sample_kernel(q_ref, k_ref, v_ref, gr_ref, zs_ref, f2p_ref, fb_ref, ong_ref, s0_ref, og_ref, sout_ref):
    t = q_ref.shape[0]
    pad = lambda x: jnp.concatenate([x, jnp.zeros((LANES - t, x.shape[1]), x.dtype)], axis=0)
    zs = pad(zs_ref[...])
    rowi = lax.broadcasted_iota(jnp.int32, (LANES, GLA_DK), 0)
    for h in range(GLA_HEADS):
        qk = slice(h * GLA_DK, (h + 1) * GLA_DK)
        vv = slice(h * GLA_DV, (h + 1) * GLA_DV)
        q = pad(q_ref[:, qk] * (GLA_DK ** -0.5))
        k = pad(k_ref[:, qk])
        v = pad(v_ref[:, vv]).astype(BF16)
        fb = jnp.where(rowi < t, fb_ref[h], 1e4)
        o, s_next = _gla_chunk(q, k, v, zs, f2p_ref[h], fb, s0_ref[h], 1, t)
        og_ref[:, vv] = _gla_out(o, gr_ref[:, vv], ong_ref[...])
        sout_ref[h] = s_next


def _gla_sample(z1s, zss, f2p, fb, ong, s0, n_new):
    ts = z1s.shape[0]
    b = ts // n_new
    whole = lambda bi: (0, 0, 0)
    state_spec = pl.BlockSpec((None, GLA_HEADS, GLA_DK, GLA_DV), lambda bi: (bi, 0, 0, 0))
    return pl.pallas_call(
        _gla_sample_kernel,
        out_shape=(jax.ShapeDtypeStruct((ts, GLA_V_W), F32),
                   jax.ShapeDtypeStruct((b, GLA_HEADS, GLA_DK, GLA_DV), F32)),
        grid=(b,),
        in_specs=[pl.BlockSpec((n_new, GLA_QK_W), lambda bi: (bi, 0)),
                  pl.BlockSpec((n_new, GLA_QK_W), lambda bi: (bi, 1)),
                  pl.BlockSpec((n_new, GLA_V_W), lambda bi: (bi, 1)),
                  pl.BlockSpec((n_new, GLA_V_W), lambda bi: (bi, 2)),
                  pl.BlockSpec((n_new, LANES), lambda bi: (bi, 0)),
                  pl.BlockSpec((GLA_HEADS, LANES, GLA_DK), whole),
                  pl.BlockSpec((GLA_HEADS, 1, GLA_DK), whole),
                  pl.BlockSpec((1, GLA_DV), lambda bi: (0, 0)),
                  state_spec],
        out_specs=(pl.BlockSpec((n_new, GLA_V_W), lambda bi: (bi, 0)), state_spec),
        compiler_params=_cparams(("parallel",)),
        name="gla_sample",
    )(z1s, z1s, z1s, z1s, zss, f2p, fb, ong, s0)


def _merge_kernel(ogp_ref, ogs_ref, ofp_ref, ofs_ref, gate_ref, wpa_ref, wpb_ref, o_ref, *, n_prompt_blocks):
    i = pl.program_id(0)

    def run(og, of):
        d = o_ref.shape[1]
        ma = gate_ref[:, :d].astype(F32)
        mb = gate_ref[:, d:].astype(F32)
        a = jnp.dot(og, wpa_ref[...], preferred_element_type=F32)
        b = jnp.dot(of, wpb_ref[...], preferred_element_type=F32)
        o_ref[...] = (jax.nn.sigmoid(ma) * a + jax.nn.sigmoid(mb) * b).astype(o_ref.dtype)

    @pl.when(i < n_prompt_blocks)
    def _():
        run(ogp_ref[...], ofp_ref[...])

    @pl.when(i >= n_prompt_blocks)
    def _():
        run(ogs_ref[...], ofs_ref[...])


def _merge(ogp, ogs, ofp, ofs, z3, wpa, wpb):
    tp, ts = ogp.shape[0], ogs.shape[0]
    d = wpa.shape[1]
    tb = ROW_BLOCK
    npb, nb = tp // tb, (tp + ts) // tb
    pidx = lambda i: (jnp.minimum(i, npb - 1), 0)
    sidx = lambda i: (jnp.maximum(i - npb, 0), 0)
    const = lambda i: (0, 0)
    return pl.pallas_call(
        functools.partial(_merge_kernel, n_prompt_blocks=npb),
        out_shape=jax.ShapeDtypeStruct((tp + ts, d), BF16),
        grid=(nb,),
        in_specs=[pl.BlockSpec((tb, GLA_V_W), pidx), pl.BlockSpec((tb, GLA_V_W), sidx),
                  pl.BlockSpec((tb, FOX_W), pidx), pl.BlockSpec((tb, FOX_W), sidx),
                  pl.BlockSpec((tb, 2 * d), lambda i: (i, 0)),
                  pl.BlockSpec(wpa.shape, const), pl.BlockSpec(wpb.shape, const)],
        out_specs=pl.BlockSpec((tb, d), lambda i: (i, 0)),
        compiler_params=_cparams(("parallel",)),
        name="merge",
    )(ogp, ogs, ofp, ofs, z3, wpa, wpb)


def _outproj_kernel(m_ref, w_ref, xp_ref, xs_ref, g1p_ref, g1s_ref, n2g_ref, scp_ref, shp_ref, scs_ref, shs_ref,
                    x1_ref, h2_ref, *, n_prompt_blocks):
    i = pl.program_id(0)
    y = jnp.dot(m_ref[...], w_ref[...], preferred_element_type=F32)

    def run(x, g1, sc, sh):
        x1 = x + g1 * y
        x1_ref[...] = x1
        h2_ref[...] = (_rms(x1, n2g_ref[...]) * (1.0 + sc) + sh).astype(h2_ref.dtype)

    @pl.when(i < n_prompt_blocks)
    def _():
        run(xp_ref[...], g1p_ref[...], scp_ref[...], shp_ref[...])

    @pl.when(i >= n_prompt_blocks)
    def _():
        run(xs_ref[...], g1s_ref[...], scs_ref[...], shs_ref[...])


def _outproj(merged, wout, xp, xs, g1p, g1s, n2g, scp, shp, scs, shs):
    tp, d = xp.shape
    ts = xs.shape[0]
    tb = ROW_BLOCK
    npb, nb = tp // tb, (tp + ts) // tb
    pidx = lambda i: (jnp.minimum(i, npb - 1), 0)
    sidx = lambda i: (jnp.maximum(i - npb, 0), 0)
    const = lambda i: (0, 0)
    row = pl.BlockSpec((1, d), const)
    srow = pl.BlockSpec((tb, d), sidx)
    return pl.pallas_call(
        functools.partial(_outproj_kernel, n_prompt_blocks=npb),
        out_shape=(jax.ShapeDtypeStruct((tp + ts, d), F32), jax.ShapeDtypeStruct((tp + ts, d), BF16)),
        grid=(nb,),
        in_specs=[pl.BlockSpec((tb, d), lambda i: (i, 0)), pl.BlockSpec(wout.shape, const),
                  pl.BlockSpec((tb, d), pidx), srow, row, srow, row, row, row, srow, srow],
        out_specs=(pl.BlockSpec((tb, d), lambda i: (i, 0)), pl.BlockSpec((tb, d), lambda i: (i, 0))),
        compiler_params=_cparams(("parallel",)),
        name="out_proj",
    )(merged, wout, xp, xs, g1p, g1s, n2g, scp, shp, scs, shs)


def _top_rows(x, n):
    row = lax.broadcasted_iota(jnp.int32, x.shape, 0)
    out = []
    for _ in range(n):
        m = jnp.max(x, axis=0, keepdims=True)
        out.append(m)
        first = jnp.min(jnp.where(x == m, row, x.shape[0]), axis=0, keepdims=True)
        x = jnp.where(row == first, NEG_INF, x)
    return out


def _top_rows_distinct(x, n):
    out = []
    for _ in range(n):
        m = jnp.max(x, axis=0, keepdims=True)
        out.append(m)
        x = jnp.where(x == m, NEG_INF, x)
    used = jnp.sum(jnp.where(x == NEG_INF, 1.0, 0.0), axis=0, keepdims=True)
    return out, used


def _peer_topk_kernel(q_ref, sk_ref, th_ref, lk_ref, s2_ref):
    n = PEER_TOPK + 1
    scores = []
    for hp in range(2 * PEER_HEADS):
        c0 = hp * PEER_NKEYS
        scores.append(lax.dot_general(sk_ref[hp], q_ref[:, c0:c0 + PEER_NKEYS], NT,
                                      preferred_element_type=F32))
    fast = [_top_rows_distinct(s, n) for s in scores]
    most_used = jnp.max(jnp.concatenate([used for _, used in fast], axis=0))
    tops = lax.cond(most_used <= n,
                    lambda: [vals for vals, _ in fast],
                    lambda: [_top_rows(s, n) for s in scores])
    for h in range(PEER_HEADS):
        s = scores[2 * h:2 * h + 2]
        v1, v2 = tops[2 * h], tops[2 * h + 1]
        cands = [v1[a] + v2[b] for a in range(n) for b in range(n) if (a + 1) * (b + 1) <= n]
        cands += [jnp.full_like(v1[0], NEG_INF)] * (-len(cands) % 8)
        sel = _top_rows(jnp.concatenate(cands, axis=0), n)
        zsum = sel[0] * 0.0 + 1.0
        for r in range(1, PEER_TOPK):
            zsum = zsum + jnp.exp(sel[r] - sel[0])
        tau = 0.5 * (sel[PEER_TOPK - 1] + sel[PEER_TOPK])
        th_ref[h] = tau - s[0]
        lk_ref[h] = s[0] - (v1[0] + v2[0] + jnp.log(zsum))
        s2_ref[h] = s[1]


def _peer_topk(q, sk):
    t = q.shape[0]
    tb = ROW_BLOCK
    shp = jax.ShapeDtypeStruct((PEER_HEADS, PEER_NKEYS, t), F32)
    ospec = pl.BlockSpec((PEER_HEADS, PEER_NKEYS, tb), lambda i: (0, 0, i))
    return pl.pallas_call(
        _peer_topk_kernel,
        out_shape=(shp, shp, shp),
        grid=(t // tb,),
        in_specs=[pl.BlockSpec((tb, q.shape[1]), lambda i: (i, 0)),
                  pl.BlockSpec(sk.shape, lambda i: (0, 0, 0))],
        out_specs=(ospec, ospec, ospec),
        compiler_params=_cparams(("parallel",)),
        name="peer_topk",
    )(q, sk)


PEER_TOKEN_CHUNK = 256


def _peer_main_kernel(h2t_ref, u_ref, vt_ref, th_ref, lk_ref, s2_ref, o_ref):
    j = pl.program_id(1)
    te = u_ref.shape[0]
    tc = PEER_TOKEN_CHUNK
    n_chunks = h2t_ref.shape[1] // tc

    @pl.when(j == 0)
    def _():
        o_ref[...] = jnp.zeros_like(o_ref)

    def activations(c):
        return jnp.dot(u_ref[...], h2t_ref[:, c * tc:(c + 1) * tc], preferred_element_type=F32)

    def gated(c, at):
        cols = slice(c * tc, (c + 1) * tc)
        ws = []
        for a in range(te // PEER_NKEYS):
            x = at[a * PEER_NKEYS:(a + 1) * PEER_NKEYS]
            ge = 0.5 * x * (1.0 + lax.erf(x * (2.0 ** -0.5)))
            g = jnp.zeros_like(x)
            for h in range(PEER_HEADS):
                s2 = s2_ref[h, :, cols]
                g = g + jnp.where(s2 >= th_ref[h, a:a + 1, cols], jnp.exp(s2 + lk_ref[h, a:a + 1, cols]), 0.0)
            ws.append((g * ge).astype(BF16))
        return jnp.concatenate(ws, axis=0)

    at = activations(0)
    for c in range(n_chunks):
        at_next = activations(c + 1) if c + 1 < n_chunks else None
        w = gated(c, at)
        o_ref[:, c * tc:(c + 1) * tc] += jnp.dot(vt_ref[...], w, preferred_element_type=F32)
        at = at_next


def _peer_main(h2t, u, vt, th, lk, s2):
    d, t = h2t.shape
    n_exp = u.shape[0]
    tm = _pick(t, (768, 512, 256))
    te = 1024
    a_per = te // PEER_NKEYS
    sel_spec = pl.BlockSpec((PEER_HEADS, a_per, tm), lambda i, j: (0, j, i))
    tok_spec = pl.BlockSpec((PEER_HEADS, PEER_NKEYS, tm), lambda i, j: (0, 0, i))
    return pl.pallas_call(
        _peer_main_kernel,
        out_shape=jax.ShapeDtypeStruct((d, t), F32),
        grid=(t // tm, n_exp // te),
        in_specs=[pl.BlockSpec((d, tm), lambda i, j: (0, i)),
                  pl.BlockSpec((te, d), lambda i, j: (j, 0)),
                  pl.BlockSpec((d, te), lambda i, j: (0, j)),
                  sel_spec, sel_spec, tok_spec],
        out_specs=pl.BlockSpec((d, tm), lambda i, j: (0, i)),
        compiler_params=_cparams(("parallel", "arbitrary")),
        name="peer_main",
    )(h2t, u, vt, th, lk, s2)


def _final_kernel(x1_ref, yt_ref, g2p_ref, g2s_ref, op_ref, os_ref, *, n_prompt_blocks):
    i = pl.program_id(0)
    y = yt_ref[...].T

    @pl.when(i < n_prompt_blocks)
    def _():
        op_ref[...] = x1_ref[...] + g2p_ref[...] * y

    @pl.when(i >= n_prompt_blocks)
    def _():
        os_ref[...] = x1_ref[...] + g2s_ref[...] * y


def _final(x1, yt, g2p, g2s, tp):
    t, d = x1.shape
    ts = t - tp
    tb = ROW_BLOCK
    npb, nb = tp // tb, t // tb
    pidx = lambda i: (jnp.minimum(i, npb - 1), 0)
    sidx = lambda i: (jnp.maximum(i - npb, 0), 0)
    return pl.pallas_call(
        functools.partial(_final_kernel, n_prompt_blocks=npb),
        out_shape=(jax.ShapeDtypeStruct((tp, d), F32), jax.ShapeDtypeStruct((ts, d), F32)),
        grid=(nb,),
        in_specs=[pl.BlockSpec((tb, d), lambda i: (i, 0)), pl.BlockSpec((d, tb), lambda i: (0, i)),
                  pl.BlockSpec((1, d), lambda i: (0, 0)), pl.BlockSpec((tb, d), sidx)],
        out_specs=(pl.BlockSpec((tb, d), pidx), pl.BlockSpec((tb, d), sidx)),
        compiler_params=_cparams(("arbitrary",)),
        name="final_residual",
    )(x1, yt, g2p, g2s)


def _layer(layer, xp, xs, c_prompt, c_sample, state_gla, cache_k, cache_v, cache_logf, page_table,
           norm1_g, norm2_g, w_ada, b_ada, w_in, gla_f2, gla_fb, gla_on_g, fox_qn_g, fox_kn_g, fox_fb,
           w_pa, w_pb, w_out, peer_wq, peer_subkeys, peer_u, peer_v):
    bp, tpp, d = xp.shape
    assert bp == 1 and d == D_MODEL
    b, n_new, _ = xs.shape
    tp, ts = bp * tpp, b * n_new
    assert tp % ROW_BLOCK == 0 and ts % ROW_BLOCK == 0
    xp2 = xp.reshape(tp, d)
    xs2 = xs.reshape(ts, d)

    n_c = 1 + b
    n_c_pad = -(-n_c // 8) * 8
    c_all = jnp.concatenate([c_prompt, c_sample, jnp.zeros((n_c_pad - n_c, d), F32)], axis=0)
    mod = _ada(c_all, w_ada, b_ada)
    mod_p = [mod[0:1, i * d:(i + 1) * d] for i in range(6)]
    mod_s = [jnp.repeat(mod[1:n_c, i * d:(i + 1) * d], n_new, axis=0) for i in range(6)]
    sh1p, sc1p, g1p, sh2p, sc2p, g2p = mod_p
    sh1s, sc1s, g1s, sh2s, sc2s, g2s = mod_s

    h = _normmod(xp2, xs2, norm1_g.reshape(1, d), sc1p, sh1p, sc1s, sh1s)

    w1 = jnp.concatenate([w_in[:, :_OFF_GF], w_in[:, _OFF_FQ:_OFF_FK]], axis=1).astype(BF16)
    w2 = w_in[:, _OFF_FK:_OFF_FF].astype(BF16)
    w3 = w_in[:, _OFF_MA:_OFF_END].astype(BF16)
    w4 = jnp.concatenate([w_in[:, _OFF_FF:_OFF_MA], w_in[:, _OFF_GF:_OFF_FQ],
                          jnp.zeros((d, LANES - FOX_HEADS - GLA_GATE_RANK), F32)], axis=1).astype(BF16)
    z1 = _matmul(h, w1, BF16, "in_proj_mixers")
    z2 = _matmul(h, w2, F32, "in_proj_fox_kv")
    z3 = _matmul(h, w3, BF16, "in_proj_gates")
    zs = _matmul(h, w4, F32, "in_proj_small")

    fb_pad = jnp.concatenate([fox_fb, jnp.zeros((LANES - FOX_HEADS,), F32)]).reshape(1, LANES)
    (qn, kb, vb, cum, k_p, v_p, lf_p, k_s, v_s, lf_s) = _fox_prep(
        z1, z2, zs, fox_qn_g.reshape(1, FOX_HD), fox_kn_g.reshape(1, FOX_HD), fb_pad, tp, n_new)
    neg_cum_t = (-LOG2E * cum[:tp, :FOX_HEADS]).T.reshape(FOX_HEADS, 1, tp)
    of_p = _fox_prompt(qn, kb, vb, neg_cum_t, tp)

    qs = qn[tp:].reshape(b, n_new, FOX_HEADS, FOX_HD)
    eye = jnp.eye(FOX_HEADS, dtype=BF16)
    qbd = (qs.transpose(0, 2, 3, 1)[:, :, :, None, :] * eye[None, :, None, :, None])
    qbd = qbd.reshape(b, FOX_W, FOX_HEADS * n_new)
    qbd = jnp.concatenate([qbd, jnp.zeros((b, FOX_W, LANES - FOX_HEADS * n_new), BF16)], axis=2)
    lfn_t = lf_s[:, :FOX_HEADS].reshape(b, n_new, FOX_HEADS).transpose(0, 2, 1)
    lfn_t = jnp.concatenate([lfn_t, jnp.zeros((b, FOX_HEADS, LANES - n_new), F32)], axis=2)
    of_s = _fox_sample(layer, page_table, qbd, k_s.reshape(b, n_new, FOX_W), v_s.reshape(b, n_new, FOX_W), lfn_t,
                       cache_k, cache_v, cache_logf).reshape(ts, FOX_W).astype(BF16)

    f2p = jnp.zeros((LANES, GLA_QK_W), F32).at[FOX_HEADS:FOX_HEADS + GLA_GATE_RANK].set(gla_f2)
    f2p = f2p.reshape(LANES, GLA_HEADS, GLA_DK).transpose(1, 0, 2).astype(BF16)
    fbh = gla_fb.reshape(GLA_HEADS, 1, GLA_DK)
    ong = gla_on_g.reshape(1, GLA_DV)
    og_p, sg_p = _gla_prompt(z1, zs, f2p, fbh, ong, jnp.zeros((GLA_HEADS, GLA_DK, GLA_DV), F32), tp)
    og_s, sg_s = _gla_sample(z1[tp:].astype(F32), zs[tp:], f2p, fbh, ong, state_gla, n_new)

    merged = _merge(og_p, og_s.astype(BF16), of_p, of_s, z3, w_pa.astype(BF16), w_pb.astype(BF16))
    x1, h2 = _outproj(merged, w_out.astype(BF16), xp2, xs2, g1p, g1s, norm2_g.reshape(1, d),
                      sc2p, sh2p, sc2s, sh2s)

    q = _matmul(h2, peer_wq.astype(BF16), BF16, "peer_query")
    sk = peer_subkeys.reshape(PEER_HEADS * 2, PEER_NKEYS, -1).astype(BF16)
    th, lk, s2 = _peer_topk(q, sk)
    yt = _peer_main(h2.T, peer_u.astype(BF16), peer_v.astype(BF16).T, th, lk, s2)
    out_p, out_s = _final(x1, yt, g2p, g2s, tp)

    return (out_p.reshape(bp, tpp, d), out_s.reshape(b, n_new, d),
            sg_p.reshape(bp, GLA_HEADS, GLA_DK, GLA_DV), sg_s,
            k_p.reshape(bp, tpp, FOX_HEADS, FOX_HD), v_p.reshape(bp, tpp, FOX_HEADS, FOX_HD),
            lf_p[:, :FOX_HEADS].reshape(bp, tpp, FOX_HEADS),
            k_s.reshape(b, n_new, FOX_HEADS, FOX_HD), v_s.reshape(b, n_new, FOX_HEADS, FOX_HD),
            lf_s[:, :FOX_HEADS].reshape(b, n_new, FOX_HEADS))


def kernel(x_prompt, x_sample, c_prompt, c_sample, state_gla, cache_k, cache_v, cache_logf, page_table,
           norm1_g, norm2_g, w_ada, b_ada, w_in, gla_f2, gla_fb, gla_on_g, fox_qn_g, fox_kn_g, fox_fb,
           w_pa, w_pb, w_out, peer_wq, peer_subkeys, peer_u, peer_v):
    depth = w_in.shape[0]
    xp, xs = x_prompt, x_sample
    outs = []
    for l in range(depth):
        res = _layer(l, xp, xs, c_prompt, c_sample, state_gla[l], cache_k, cache_v, cache_logf, page_table,
                     norm1_g[l], norm2_g[l], w_ada[l], b_ada[l], w_in[l], gla_f2[l], gla_fb[l], gla_on_g[l],
                     fox_qn_g[l], fox_kn_g[l], fox_fb[l], w_pa[l], w_pb[l], w_out[l],
                     peer_wq[l], peer_subkeys[l], peer_u[l], peer_v[l])
        xp, xs = res[0], res[1]
        outs.append(res[2:])
    stacked = [jnp.stack([o[i] for o in outs]) for i in range(8)]
    return (xp, xs, *stacked)
```
